```python
import math
import jax, jax.numpy as jnp
from jax import lax
import numpy as np

D_MODEL = 1024
BATCH = 8
SEQ = 4096
DEPTH = 2

BLOCK_Q = 128
RMS_EPS = 1e-6
SB_HEADS = 8
SB_HEAD_DIM = 64
SB_WIDTH = SB_HEADS * SB_HEAD_DIM
MLA_HEADS = 8
MLA_NOPE_DIM = 64
MLA_ROPE_DIM = 32
MLA_V_DIM = 64
MLA_Q_LORA = 384
MLA_KV_LORA = 256
MLA_WIDTH = MLA_HEADS * MLA_V_DIM
ROPE_THETA = 10000.0
FOX_HEADS = 16
FOX_HEAD_DIM = 64
FOX_WIDTH = FOX_HEADS * FOX_HEAD_DIM
EVEN_IN_WIDTH = 4 * SB_WIDTH + MLA_Q_LORA + MLA_KV_LORA + MLA_ROPE_DIM + MLA_WIDTH
ODD_IN_WIDTH = 4 * FOX_WIDTH + FOX_HEADS

kernel_name = "hybrid_stickbreak_mla_fox_sandwich"


def rms_norm(x, g):
    xf = x.astype(jnp.float32)
    var = jnp.mean(xf * xf, axis=-1, keepdims=True)
    return (xf * lax.rsqrt(var + RMS_EPS)).astype(x.dtype) * g


def split_heads(t, n_heads):
    b, s, _ = t.shape
    return t.reshape(b, s, n_heads, -1).transpose(0, 2, 1, 3)


def merge_heads(o):
    b, h, s, d = o.shape
    return o.transpose(0, 2, 1, 3).reshape(b, s, h * d)


def sweep_blocks(block_fn, n_blocks):
    out = lax.map(block_fn, jnp.arange(n_blocks))
    nb, b, h, bq, d = out.shape
    return out.transpose(1, 2, 0, 3, 4).reshape(b, h, nb * bq, d)


def rope_angles(positions, dim):
    inv_freq = ROPE_THETA ** (-jnp.arange(0, dim, 2, dtype=jnp.float32) / dim)
    ang = positions.astype(jnp.float32)[..., None] * inv_freq
    return jnp.cos(ang), jnp.sin(ang)


def apply_rope(x, cos, sin):
    x1, x2 = jnp.split(x, 2, axis=-1)
    cos = cos.astype(x.dtype)
    sin = sin.astype(x.dtype)
    return jnp.concatenate([x1 * cos - x2 * sin, x2 * cos + x1 * sin], axis=-1)


def stick_breaking_attention(q, k, v):
    s_len, d = q.shape[2], q.shape[3]
    scale = d ** -0.5
    k_pos = jnp.arange(s_len)

    def one_block(i):
        start = i * BLOCK_Q
        qb = lax.dynamic_slice_in_dim(q, start, BLOCK_Q, axis=2)
        z = jnp.einsum('bhqd,bhkd->bhqk', qb, k).astype(jnp.float32) * scale
        q_pos = start + jnp.arange(BLOCK_Q)
        before = k_pos[None, :] < q_pos[:, None]
        log_keep = jnp.where(before, jax.nn.log_sigmoid(-z), 0.0)
        log_remain = lax.cumsum(log_keep, axis=3, reverse=True) - log_keep
        w = jnp.where(before, jnp.exp(jax.nn.log_sigmoid(z) + log_remain), 0.0)
        return jnp.einsum('bhqk,bhkd->bhqd', w.astype(v.dtype), v)

    return sweep_blocks(one_block, s_len // BLOCK_Q)


def mla_attention(q_nope, q_rope, k_nope, k_rope, v):
    s_len = q_nope.shape[2]
    scale = (MLA_NOPE_DIM + MLA_ROPE_DIM) ** -0.5
    k_pos = jnp.arange(s_len)

    def one_block(i):
        start = i * BLOCK_Q
        qn = lax.dynamic_slice_in_dim(q_nope, start, BLOCK_Q, axis=2)
        qr = lax.dynamic_slice_in_dim(q_rope, start, BLOCK_Q, axis=2)
        z = (jnp.einsum('bhqd,bhkd->bhqk', qn, k_nope)
             + jnp.einsum('bhqr,bkr->bhqk', qr, k_rope)).astype(jnp.float32) * scale
        q_pos = start + jnp.arange(BLOCK_Q)
        causal = k_pos[None, :] <= q_pos[:, None]
        p = jax.nn.softmax(jnp.where(causal, z, -jnp.inf), axis=-1)
        return jnp.einsum('bhqk,bhkd->bhqd', p.astype(v.dtype), v)

    return sweep_blocks(one_block, s_len // BLOCK_Q)


def forgetting_attention(q, k, v, log_f):
    s_len, d = q.shape[2], q.shape[3]
    scale = d ** -0.5
    c = lax.cumsum(log_f, axis=2)
    k_pos = jnp.arange(s_len)

    def one_block(i):
        start = i * BLOCK_Q
        qb = lax.dynamic_slice_in_dim(q, start, BLOCK_Q, axis=2)
        cq = lax.dynamic_slice_in_dim(c, start, BLOCK_Q, axis=2)
        z = jnp.einsum('bhqd,bhkd->bhqk', qb, k).astype(jnp.float32) * scale
        z = z + cq[..., :, None] - c[..., None, :]
        q_pos = start + jnp.arange(BLOCK_Q)
        causal = k_pos[None, :] <= q_pos[:, None]
        p = jax.nn.softmax(jnp.where(causal, z, -jnp.inf), axis=-1)
        return jnp.einsum('bhqk,bhkd->bhqd', p.astype(v.dtype), v)

    return sweep_blocks(one_block, s_len // BLOCK_Q)


def even_layer(x, positions, pre_g, post_g, w_in, q_a_g, w_q_b, kv_a_g, w_kv_b, w_out):
    b, s, _ = x.shape
    h = rms_norm(x, pre_g)
    proj = h @ w_in
    cuts = [SB_WIDTH, 2 * SB_WIDTH, 3 * SB_WIDTH, 4 * SB_WIDTH,
            4 * SB_WIDTH + MLA_Q_LORA,
            4 * SB_WIDTH + MLA_Q_LORA + MLA_KV_LORA + MLA_ROPE_DIM]
    sb_q, sb_k, sb_v, sb_gate, q_a, kv_a, mla_gate = jnp.split(proj, cuts, axis=-1)

    o_a = stick_breaking_attention(split_heads(sb_q, SB_HEADS), split_heads(sb_k, SB_HEADS),
                                   split_heads(sb_v, SB_HEADS))
    o_a = merge_heads(o_a) * jax.nn.silu(sb_gate)

    q = (rms_norm(q_a, q_a_g) @ w_q_b).reshape(b, s, MLA_HEADS, MLA_NOPE_DIM + MLA_ROPE_DIM)
    q = q.transpose(0, 2, 1, 3)
    q_nope, q_rope = q[..., :MLA_NOPE_DIM], q[..., MLA_NOPE_DIM:]
    c_kv, k_rope = kv_a[..., :MLA_KV_LORA], kv_a[..., MLA_KV_LORA:]
    kv = (rms_norm(c_kv, kv_a_g) @ w_kv_b).reshape(b, s, MLA_HEADS, MLA_NOPE_DIM + MLA_V_DIM)
    kv = kv.transpose(0, 2, 1, 3)
    k_nope, v = kv[..., :MLA_NOPE_DIM], kv[..., MLA_NOPE_DIM:]
    cos, sin = rope_angles(positions, MLA_ROPE_DIM)
    q_rope = apply_rope(q_rope, cos[:, None], sin[:, None])
    k_rope = apply_rope(k_rope, cos, sin)
    o_b = mla_attention(q_nope, q_rope, k_nope, k_rope, v)
    o_b = merge_heads(o_b) * jax.nn.silu(mla_gate)

    y = jnp.concatenate([o_a, o_b], axis=-1) @ w_out
    return x + rms_norm(y, post_g)


def odd_layer(x, pre_g, post_g, w_in, b_f, w_out):
    h = rms_norm(x, pre_g)
    proj = h @ w_in
    cuts = [FOX_WIDTH, 2 * FOX_WIDTH, 3 * FOX_WIDTH, 4 * FOX_WIDTH]
    q, k, v, gate, f_logit = jnp.split(proj, cuts, axis=-1)
    log_f = jax.nn.log_sigmoid(f_logit.astype(jnp.float32) + b_f.astype(jnp.float32))
    log_f = log_f.transpose(0, 2, 1)
    o = forgetting_attention(split_heads(q, FOX_HEADS), split_heads(k, FOX_HEADS),
                             split_heads(v, FOX_HEADS), log_f)
    y = (merge_heads(o) * jax.nn.silu(gate)) @ w_out
    return x + rms_norm(y, post_g)


def setup_inputs(seed: int = 0) -> dict:
    key = jax.random.key(seed)
    ks = jax.random.split(key, 20)

    def w(k, shape):
        return jax.random.normal(k, shape, jnp.float32) * (shape[0] ** -0.5)

    def gain(k, n):
        return 1.0 + 0.02 * jax.random.normal(k, (n,), jnp.float32)

    x = jax.random.normal(ks[0], (BATCH, SEQ, D_MODEL), jnp.float32)
    positions = jnp.broadcast_to(jnp.arange(SEQ, dtype=jnp.int32)[None, :], (BATCH, SEQ))
    return {
        "x": x,
        "positions": positions,
        "l0_pre_g": gain(ks[1], D_MODEL),
        "l0_post_g": gain(ks[2], D_MODEL),
        "l0_w_in": w(ks[3], (D_MODEL, EVEN_IN_WIDTH)),
        "l0_q_a_g": gain(ks[4], MLA_Q_LORA),
        "l0_w_q_b": w(ks[5], (MLA_Q_LORA, MLA_HEADS * (MLA_NOPE_DIM + MLA_ROPE_DIM))),
        "l0_kv_a_g": gain(ks[6], MLA_KV_LORA),
        "l0_w_kv_b": w(ks[7], (MLA_KV_LORA, MLA_HEADS * (MLA_NOPE_DIM + MLA_V_DIM))),
        "l0_w_out": w(ks[8], (SB_WIDTH + MLA_WIDTH, D_MODEL)),
        "l1_pre_g": gain(ks[9], D_MODEL),
        "l1_post_g": gain(ks[10], D_MODEL),
        "l1_w_in": w(ks[11], (D_MODEL, ODD_IN_WIDTH)),
        "l1_b_f": 2.0 + 0.5 * jax.random.normal(ks[12], (FOX_HEADS,), jnp.float32),
        "l1_w_out": w(ks[13], (FOX_WIDTH, D_MODEL)),
    }


def reference(x, positions, l0_pre_g, l0_post_g, l0_w_in, l0_q_a_g, l0_w_q_b, l0_kv_a_g,
              l0_w_kv_b, l0_w_out, l1_pre_g, l1_post_g, l1_w_in, l1_b_f, l1_w_out):
    layer_params = [
        (l0_pre_g, l0_post_g, l0_w_in, l0_q_a_g, l0_w_q_b, l0_kv_a_g, l0_w_kv_b, l0_w_out),
        (l1_pre_g, l1_post_g, l1_w_in, l1_b_f, l1_w_out),
    ]
    for layer in range(DEPTH):
        p = layer_params[layer]
        if layer % 2 == 0:
            x = even_layer(x, positions, *p)
        else:
            x = odd_layer(x, *p)
    return x
```

```python
import functools

import jax
import jax.numpy as jnp
from jax import lax
from jax.experimental import pallas as pl
from jax.experimental.pallas import tpu as pltpu

D_MODEL = 1024
RMS_EPS = 1e-6
HEAD_DIM = 64
SB_HEADS = 8
SB_WIDTH = SB_HEADS * HEAD_DIM
MLA_HEADS = 8
MLA_NOPE = 64
MLA_ROPE = 32
MLA_V = 64
MLA_Q_LORA = 384
MLA_KV_LORA = 256
MLA_WIDTH = MLA_HEADS * MLA_V
ROPE_THETA = 10000.0
FOX_HEADS = 16
FOX_WIDTH = FOX_HEADS * HEAD_DIM

LANES = 128
ROW_TILE = 256
Q_TILE = 256
K_TILE = 256
MASK_VALUE = -1e30
VMEM_LIMIT = 48 * 1024 * 1024

F32 = jnp.float32
BF16 = jnp.bfloat16


def _rms(xf, g):
    var = jnp.mean(xf * xf, axis=-1, keepdims=True)
    return xf * lax.rsqrt(var + RMS_EPS) * g


def _dot(a, b):
    return jnp.dot(a, b, preferred_element_type=F32)


def _dot_nt(a, b):
    return lax.dot_general(a, b, (((1,), (1,)), ((), ())), preferred_element_type=F32)


def _softplus(z):
    return jnp.maximum(z, 0.0) + jnp.log1p(jnp.exp(-jnp.abs(z)))


def _silu(g):
    return g / (1.0 + jnp.exp(-g))


def _l0_in_kernel(x_ref, posr_ref, posc_ref, preg_ref, wn_ref, wt_ref, qag_ref, wqb_ref,
                  kvagr_ref, kvagc_ref, wkT_ref, wv_ref, invfr_ref, invfc_ref,
                  qsb_ref, ktsb_ref, vsb_ref, gate_ref, qmla_ref, ktmla_ref, vmla_ref):
    x = x_ref[0]
    h = _rms(x, preg_ref[...]).astype(BF16)

    def nat(a, b):
        return _dot(h, wn_ref[:, a:b])

    qsb_ref[0] = nat(0, 512).astype(BF16)
    vsb_ref[0] = nat(512, 1024).astype(BF16)
    gate_ref[0] = nat(1024, 2048).astype(BF16)
    qa = nat(2048, 2432)
    ckv = nat(2432, 2688)

    tr = _dot_nt(wt_ref[...], h)
    tm = tr.shape[1]
    zeros64 = jnp.zeros((HEAD_DIM, tm), BF16)
    for hh in range(SB_HEADS):
        off = HEAD_DIM * (hh % 2)
        ktsb_ref[0, hh, off:off + HEAD_DIM, :] = tr[hh * HEAD_DIM:(hh + 1) * HEAD_DIM].astype(BF16)
        ktsb_ref[0, hh, HEAD_DIM - off:2 * HEAD_DIM - off, :] = zeros64

    qan = _rms(qa, qag_ref[...]).astype(BF16)
    qq = _dot(qan, wqb_ref[...])
    ang = posc_ref[0].astype(F32) * invfr_ref[...]
    cos = jnp.cos(ang)
    sin = jnp.sin(ang)
    qmla_ref[0, :, 0:512] = qq[:, 0:512].astype(BF16)
    for c in range(2):
        qr = qq[:, 512 + c * LANES:512 + (c + 1) * LANES]
        qrr = qq[:, 768 + c * LANES:768 + (c + 1) * LANES]
        qmla_ref[0, :, 512 + c * LANES:512 + (c + 1) * LANES] = (qr * cos + qrr * sin).astype(BF16)

    ckvT = tr[512:768]
    varT = jnp.mean(ckvT * ckvT, axis=0, keepdims=True)
    ckvnT = (ckvT * lax.rsqrt(varT + RMS_EPS) * kvagc_ref[...]).astype(BF16)
    knT = _dot(wkT_ref[...], ckvnT)
    krT = tr[768:800]
    angT = invfc_ref[...] * posr_ref[0].astype(F32)
    cosT = jnp.cos(angT)
    sinT = jnp.sin(angT)
    k1 = krT[0:16]
    k2 = krT[16:32]
    krf = jnp.concatenate([k1 * cosT - k2 * sinT, k2 * cosT + k1 * sinT], axis=0).astype(BF16)
    ktmla_ref[...] = jnp.zeros(ktmla_ref.shape, BF16)
    for hh in range(MLA_HEADS):
        off = MLA_NOPE * (hh % 2)
        ktmla_ref[0, hh, off:off + MLA_NOPE, :] = knT[hh * MLA_NOPE:(hh + 1) * MLA_NOPE].astype(BF16)
        roff = LANES + MLA_ROPE * (hh % 4)
        ktmla_ref[0, hh, roff:roff + MLA_ROPE, :] = krf

    ckvn = _rms(ckv, kvagr_ref[...]).astype(BF16)
    vmla_ref[0] = _dot(ckvn, wv_ref[...]).astype(BF16)


def _l0_in(x, positions, pre_g, wn, wt, qag, wqb, kvag, wkT, wv, invf, tm):
    b, s, d = x.shape
    nt = s // tm
    posr = positions.reshape(b, 1, s)
    posc = positions.reshape(b, s, 1)
    invfr = jnp.tile(invf, LANES // invf.shape[0]).reshape(1, LANES)
    invfc = invf.reshape(-1, 1)
    const = lambda shape: pl.BlockSpec(shape, lambda bi, ti: (0,) * len(shape))
    row = lambda w: pl.BlockSpec((1, tm, w), lambda bi, ti: (bi, ti, 0))
    colT = lambda hn, r: pl.BlockSpec((1, hn, r, tm), lambda bi, ti: (bi, 0, 0, ti))
    out_shape = (
        jax.ShapeDtypeStruct((b, s, SB_WIDTH), BF16),
        jax.ShapeDtypeStruct((b, SB_HEADS, LANES, s), BF16),
        jax.ShapeDtypeStruct((b, s, SB_WIDTH), BF16),
        jax.ShapeDtypeStruct((b, s, SB_WIDTH + MLA_WIDTH), BF16),
        jax.ShapeDtypeStruct((b, s, 768), BF16),
        jax.ShapeDtypeStruct((b, MLA_HEADS, 2 * LANES, s), BF16),
        jax.ShapeDtypeStruct((b, s, MLA_WIDTH), BF16),
    )
    return pl.pallas_call(
        _l0_in_kernel,
        out_shape=out_shape,
        grid=(b, nt),
        in_specs=[
            row(d),
            pl.BlockSpec((1, 1, tm), lambda bi, ti: (bi, 0, ti)),
            pl.BlockSpec((1, tm, 1), lambda bi, ti: (bi, ti, 0)),
            const((1, d)), const(wn.shape), const(wt.shape), const((1, MLA_Q_LORA)), const(wqb.shape),
            const((1, MLA_KV_LORA)), const((MLA_KV_LORA, 1)), const(wkT.shape), const(wv.shape),
            const((1, LANES)), const((MLA_ROPE // 2, 1)),
        ],
        out_specs=(row(SB_WIDTH), colT(SB_HEADS, LANES), row(SB_WIDTH), row(SB_WIDTH + MLA_WIDTH),
                   row(768), colT(MLA_HEADS, 2 * LANES), row(MLA_WIDTH)),
        compiler_params=pltpu.CompilerParams(
            dimension_semantics=("arbitrary", "arbitrary"), vmem_limit_bytes=VMEM_LIMIT),
        name="l0_in",
    )(x, posr, posc, pre_g.reshape(1, d), wn, wt, qag.reshape(1, -1), wqb,
      kvag.reshape(1, -1), kvag.reshape(-1, 1), wkT, wv, invfr, invfc)


def _l1_in_kernel(x_ref, preg_ref, wn_ref, wt_ref, bf_ref, tri_ref,
                  q_ref, kt_ref, v_ref, gate_ref, bias_ref, carry_ref):
    @pl.when(pl.program_id(1) == 0)
    def _():
        carry_ref[...] = jnp.zeros(carry_ref.shape, F32)

    x = x_ref[0]
    h = _rms(x, preg_ref[...]).astype(BF16)
    q_ref[0] = _dot(h, wn_ref[:, 0:1024]).astype(BF16)
    v_ref[0] = _dot(h, wn_ref[:, 1024:2048]).astype(BF16)
    gate_ref[0] = _dot(h, wn_ref[:, 2048:3072]).astype(BF16)

    tr = _dot_nt(wt_ref[...], h)
    tm = tr.shape[1]
    zeros64 = jnp.zeros((HEAD_DIM, tm), BF16)
    for hh in range(FOX_HEADS):
        off = HEAD_DIM * (hh % 2)
        kt_ref[0, hh, off:off + HEAD_DIM, :] = tr[hh * HEAD_DIM:(hh + 1) * HEAD_DIM].astype(BF16)
        kt_ref[0, hh, HEAD_DIM - off:2 * HEAD_DIM - off, :] = zeros64

    f = tr[FOX_WIDTH:FOX_WIDTH + FOX_HEADS] + bf_ref[...]
    logf = -_softplus(-f)
    a1 = logf.astype(BF16)
    r1 = logf - a1.astype(F32)
    a2 = r1.astype(BF16)
    a3 = (r1 - a2.astype(F32)).astype(BF16)
    tri = tri_ref[...]
    c = _dot(a1, tri) + _dot(a2, tri) + _dot(a3, tri) + carry_ref[:, 0:1]
    carry_ref[...] = jnp.broadcast_to(c[:, tm - 1:tm], carry_ref.shape)
    bias_ref[0] = -c


def _l1_in(x, pre_g, wn, wt, bf, tm):
    b, s, d = x.shape
    nt = s // tm
    tri = (lax.broadcasted_iota(jnp.int32, (tm, tm), 0)
           <= lax.broadcasted_iota(jnp.int32, (tm, tm), 1)).astype(BF16)
    const = lambda shape: pl.BlockSpec(shape, lambda bi, ti: (0,) * len(shape))
    row = lambda w: pl.BlockSpec((1, tm, w), lambda bi, ti: (bi, ti, 0))
    out_shape = (
        jax.ShapeDtypeStruct((b, s, FOX_WIDTH), BF16),
        jax.ShapeDtypeStruct((b, FOX_HEADS, LANES, s), BF16),
        jax.ShapeDtypeStruct((b, s, FOX_WIDTH), BF16),
        jax.ShapeDtypeStruct((b, s, FOX_WIDTH), BF16),
        jax.ShapeDtypeStruct((b, FOX_HEADS, s), F32),
    )
    return pl.pallas_call(
        _l1_in_kernel,
        out_shape=out_shape,
        grid=(b, nt),
        in_specs=[row(d), const((1, d)), const(wn.shape), const(wt.shape),
                  const((FOX_HEADS, 1)), const((tm, tm))],
        out_specs=(row(FOX_WIDTH),
                   pl.BlockSpec((1, FOX_HEADS, LANES, tm), lambda bi, ti: (bi, 0, 0, ti)),
                   row(FOX_WIDTH), row(FOX_WIDTH),
                   pl.BlockSpec((1, FOX_HEADS, tm), lambda bi, ti: (bi, 0, ti))),
        scratch_shapes=[pltpu.VMEM((FOX_HEADS, LANES), F32)],
        compiler_params=pltpu.CompilerParams(
            dimension_semantics=("arbitrary", "arbitrary"), vmem_limit_bytes=VMEM_LIMIT),
        name="l1_in",
    )(x, pre_g.reshape(1, d), wn, wt, bf.reshape(-1, 1), tri)


def _attn_kernel(*refs, mode, split_q, has_bias, tq, tk):
    refs = list(refs)
    q_ref = refs.pop(0)
    qr_ref = refs.pop(0) if split_q else None
    kt_ref = refs.pop(0)
    v_ref = refs.pop(0)
    bias_ref = refs.pop(0) if has_bias else None
    o_ref, acc0_ref, acc1_ref = refs
    accs = (acc0_ref, acc1_ref)

    qi = pl.program_id(2)
    if split_q:
        q = jnp.concatenate([q_ref[0], qr_ref[0]], axis=1)
    else:
        q = q_ref[0]
    for a in accs:
        a[...] = jnp.zeros(a.shape, F32)

    ndiag = tq // tk
    nfull = (qi * tq) // tk
    rows = lax.broadcasted_iota(jnp.int32, (tq, tk), 0)
    cols = lax.broadcasted_iota(jnp.int32, (tq, tk), 1)

    def scores(j, hh):
        start = pl.multiple_of(j * tk, tk)
        s = _dot(q, kt_ref[0, hh, :, pl.ds(start, tk)])
        if has_bias:
            s = s + bias_ref[0, hh, :, pl.ds(start, tk)]
        return s, start

    if mode == "softmax":
        def step(j, carry, diag):
            out = []
            for hh in range(2):
                s, start = scores(j, hh)
                if diag is not None:
                    s = jnp.where(cols + diag * tk <= rows, s, MASK_VALUE)
                m_prev, l_prev = carry[2 * hh], carry[2 * hh + 1]
                m_new = jnp.maximum(m_prev, jnp.max(s, axis=1, keepdims=True))
                alpha = jnp.exp(m_prev - m_new)
                p = jnp.exp(s - m_new)
                l_new = alpha * l_prev + jnp.sum(p, axis=1, keepdims=True)
                pv = _dot(p.astype(BF16), v_ref[0, pl.ds(start, tk), :])
                accs[hh][...] = alpha * accs[hh][...] + pv
                out += [m_new, l_new]
            return tuple(out)

        init = (jnp.full((tq, 1), MASK_VALUE, F32), jnp.zeros((tq, 1), F32)) * 2
        carry = lax.fori_loop(0, nfull, lambda j, c: step(j, c, None), init)
        for dd in range(ndiag):
            carry = step(nfull + dd, carry, dd)
        lane = lax.broadcasted_iota(jnp.int32, (tq, LANES), 1)
        o = jnp.where(lane < HEAD_DIM, acc0_ref[...] / carry[1], acc1_ref[...] / carry[3])
        o_ref[0] = o.astype(o_ref.dtype)
    else:
        upper = (lax.broadcasted_iota(jnp.int32, (tk, tk), 0)
                 > lax.broadcasted_iota(jnp.int32, (tk, tk), 1)).astype(BF16)

        def step(j, carry, diag):
            out = []
            for hh in range(2):
                z, start = scores(j, hh)
                lk = -_softplus(z)
                if diag is not None:
                    before = cols + diag * tk < rows
                    lk = jnp.where(before, lk, 0.0)
                hi = lk.astype(BF16)
                lo = (lk - hi.astype(F32)).astype(BF16)
                remain = _dot(hi, upper) + _dot(lo, upper) + carry[hh]
                w = jnp.exp(z + lk + remain)
                if diag is not None:
                    w = jnp.where(before, w, 0.0)
                accs[hh][...] += _dot(w.astype(BF16), v_ref[0, pl.ds(start, tk), :])
                out.append(carry[hh] + jnp.sum(lk, axis=1, keepdims=True))
            return tuple(out)

        carry = (jnp.zeros((tq, 1), F32),) * 2
        for dd in reversed(range(ndiag)):
            carry = step(nfull + dd, carry, dd)
        lax.fori_loop(0, nfull, lambda t, c: step(nfull - 1 - t, c, None), carry)
        lane = lax.broadcasted_iota(jnp.int32, (tq, LANES), 1)
        o_ref[0] = jnp.where(lane < HEAD_DIM, acc0_ref[...], acc1_ref[...]).astype(o_ref.dtype)


def _attention(q, kt, v, bias, *, mode, tq, tk, rope_q=None):
    b, s, width = v.shape
    heads = kt.shape[1]
    kd = kt.shape[2]
    npairs = heads // 2
    split_q = rope_q is not None
    in_specs = [pl.BlockSpec((1, tq, LANES), lambda bi, pi, qi: (bi, qi, pi))]
    args = [q]
    if split_q:
        nope_tiles = (heads * MLA_NOPE) // LANES
        in_specs.append(pl.BlockSpec((1, tq, LANES), lambda bi, pi, qi: (bi, qi, nope_tiles + pi // 2)))
        args.append(rope_q)
    in_specs.append(pl.BlockSpec((1, 2, kd, s), lambda bi, pi, qi: (bi, pi, 0, 0)))
    args.append(kt)
    in_specs.append(pl.BlockSpec((1, s, LANES), lambda bi, pi, qi: (bi, 0, pi)))
    args.append(v)
    if bias is not None:
        in_specs.append(pl.BlockSpec((1, 2, 1, s), lambda bi, pi, qi: (bi, pi, 0, 0)))
        args.append(bias)
    kernel = functools.partial(_attn_kernel, mode=mode, split_q=split_q, has_bias=bias is not None,
                               tq=tq, tk=tk)
    return pl.pallas_call(
        kernel,
        out_shape=jax.ShapeDtypeStruct((b, s, width), BF16),
        grid=(b, npairs, s // tq),
        in_specs=in_specs,
        out_specs=pl.BlockSpec((1, tq, LANES), lambda bi, pi, qi: (bi, qi, pi)),
        scratch_shapes=[pltpu.VMEM((tq, LANES), F32), pltpu.VMEM((tq, LANES), F32)],
        compiler_params=pltpu.CompilerParams(
            dimension_semantics=("arbitrary", "arbitrary", "arbitrary"), vmem_limit_bytes=VMEM_LIMIT),
        name="attn_" + mode + ("_mla" if split_q else "") + ("_bias" if bias is not None else ""),
    )(*args)


def _out_kernel(*refs, n_o):
    o_refs = refs[:n_o]
    gate_ref, x_ref, w_ref, g_ref, out_ref = refs[n_o:]
    gate = _silu(gate_ref[...].astype(F32))
    y = None
    col = 0
    for o_ref in o_refs:
        wdt = o_ref.shape[1]
        og = (o_ref[...].astype(F32) * gate[:, col:col + wdt]).astype(BF16)
        part = _dot(og, w_ref[col:col + wdt, :])
        y = part if y is None else y + part
        col += wdt
    out_ref[...] = x_ref[...] + _rms(y, g_ref[...])


def _out_stage(os, gate, x, w, g, tm):
    n, d = x.shape
    os = [o.reshape(n, o.shape[-1]) for o in os]
    in_specs = [pl.BlockSpec((tm, o.shape[1]), lambda i: (i, 0)) for o in os]
    in_specs += [pl.BlockSpec((tm, gate.shape[1]), lambda i: (i, 0)),
                 pl.BlockSpec((tm, d), lambda i: (i, 0)),
                 pl.BlockSpec(w.shape, lambda i: (0, 0)),
                 pl.BlockSpec((1, d), lambda i: (0, 0))]
    return pl.pallas_call(
        functools.partial(_out_kernel, n_o=len(os)),
        out_shape=jax.ShapeDtypeStruct((n, d), F32),
        grid=(n // tm,),
        in_specs=in_specs,
        out_specs=pl.BlockSpec((tm, d), lambda i: (i, 0)),
        compiler_params=pltpu.CompilerParams(
            dimension_semantics=("arbitrary",), vmem_limit_bytes=VMEM_LIMIT),
        name="out_stage",
    )(*os, gate, x, w, g.reshape(1, d))


def kernel(x, positions, l0_pre_g, l0_post_g, l0_w_in, l0_q_a_g, l0_w_q_b, l0_kv_a_g, l0_w_kv_b, l0_w_out,
           l1_pre_g, l1_post_g, l1_w_in, l1_b_f, l1_w_out):
    b, s, d = x.shape
    tm = min(ROW_TILE, s)
    tq = min(Q_TILE, s)
    tk = min(K_TILE, tq)
    sb_scale = HEAD_DIM ** -0.5
    mla_scale = (MLA_NOPE + MLA_ROPE) ** -0.5

    w = l0_w_in
    c0 = 4 * SB_WIDTH
    c1 = c0 + MLA_Q_LORA
    c2 = c1 + MLA_KV_LORA
    c3 = c2 + MLA_ROPE
    wn0 = jnp.concatenate([w[:, 0:512] * sb_scale, w[:, 1024:1536], w[:, 1536:2048], w[:, c3:],
                           w[:, c0:c1], w[:, c1:c2]], axis=1).astype(BF16)
    wt0 = jnp.concatenate([w[:, 512:1024], w[:, c1:c2], w[:, c2:c3]], axis=1).T.astype(BF16)
    wq = l0_w_q_b.reshape(MLA_Q_LORA, MLA_HEADS, MLA_NOPE + MLA_ROPE) * mla_scale
    wq_nope = wq[:, :, :MLA_NOPE].reshape(MLA_Q_LORA, -1)
    wq_rope = wq[:, :, MLA_NOPE:]
    half = MLA_ROPE // 2
    wq_rot = jnp.concatenate([-wq_rope[:, :, half:], wq_rope[:, :, :half]], axis=2)
    wqb = jnp.concatenate([wq_nope, wq_rope.reshape(MLA_Q_LORA, -1), wq_rot.reshape(MLA_Q_LORA, -1)],
                          axis=1).astype(BF16)
    wkv = l0_w_kv_b.reshape(MLA_KV_LORA, MLA_HEADS, MLA_NOPE + MLA_V)
    wkT = wkv[:, :, :MLA_NOPE].reshape(MLA_KV_LORA, -1).T.astype(BF16)
    wv = wkv[:, :, MLA_NOPE:].reshape(MLA_KV_LORA, -1).astype(BF16)
    invf = ROPE_THETA ** (-jnp.arange(0, MLA_ROPE, 2, dtype=F32) / MLA_ROPE)

    q_sb, kt_sb, v_sb, gate0, q_mla, kt_mla, v_mla = _l0_in(
        x, positions, l0_pre_g, wn0, wt0, l0_q_a_g, wqb, l0_kv_a_g, wkT, wv, invf, tm)
    o_sb = _attention(q_sb, kt_sb, v_sb, None, mode="stick", tq=tq, tk=tk)
    o_mla = _attention(q_mla, kt_mla, v_mla, None, mode="softmax", tq=tq, tk=tk, rope_q=q_mla)
    x1 = _out_stage([o_sb, o_mla], gate0.reshape(b * s, -1), x.reshape(b * s, d),
                    l0_w_out.astype(BF16), l0_post_g, tm)

    w1 = l1_w_in
    wn1 = jnp.concatenate([w1[:, 0:1024] * sb_scale, w1[:, 2048:3072], w1[:, 3072:4096]], axis=1).astype(BF16)
    wt1 = jnp.concatenate([w1[:, 1024:2048], w1[:, 4096:]], axis=1).T.astype(BF16)
    q1, kt1, v1, gate1, bias1 = _l1_in(x1.reshape(b, s, d), l1_pre_g, wn1, wt1, l1_b_f, tm)
    o1 = _attention(q1, kt1, v1, bias1.reshape(b, FOX_HEADS, 1, s), mode="softmax", tq=tq, tk=tk)
    out = _out_stage([o1], gate1.reshape(b * s, -1), x1, l1_w_out.astype(BF16), l1_post_g, tm)
    return out.reshape(b, s, d)
```

```python
import functools
import math

import jax
import jax.numpy as jnp
from jax import lax
from jax.experimental import pallas as pl
from jax.experimental.pallas import tpu as pltpu

D_MODEL = 1024
RMS_EPS = 1e-6
HEAD_DIM = 64
SB_HEADS = 8
SB_WIDTH = SB_HEADS * HEAD_DIM
MLA_HEADS = 8
MLA_NOPE = 64
MLA_ROPE = 32
MLA_V = 64
MLA_Q_LORA = 384
MLA_KV_LORA = 256
MLA_WIDTH = MLA_HEADS * MLA_V
ROPE_THETA = 10000.0
FOX_HEADS = 16
FOX_WIDTH = FOX_HEADS * HEAD_DIM

LANES = 128
ROW_TILE = 256
Q_TILE = 1024
SOFTMAX_CHUNK = 512
DIAG_CHUNK = 256
COLUMN_SPLIT = 1024
MASK_VALUE = -1e30
LOG2E = math.log2(math.e)
VMEM_LIMIT = 48 * 1024 * 1024

F32 = jnp.float32
BF16 = jnp.bfloat16


def _rms(xf, g):
    var = jnp.mean(xf * xf, axis=-1, keepdims=True)
    return xf * lax.rsqrt(var + RMS_EPS) * g


def _dot(a, b):
    return jnp.dot(a, b, preferred_element_type=F32)


def _dot_nt(a, b):
    return lax.dot_general(a, b, (((1,), (1,)), ((), ())), preferred_element_type=F32)


def _silu(g):
    return g / (1.0 + jnp.exp(-g))


def _split3(v):
    a1 = v.astype(BF16)
    r1 = v - a1.astype(F32)
    a2 = r1.astype(BF16)
    a3 = (r1 - a2.astype(F32)).astype(BF16)
    return a1, a2, a3


def _write_padded_heads(ref, rows_t, heads, width):
    tm = rows_t.shape[1]
    zeros = jnp.zeros((HEAD_DIM, tm), BF16)
    for hh in range(heads):
        off = HEAD_DIM * (hh % 2)
        ref[0, hh, off:off + HEAD_DIM, :] = rows_t[hh * width:hh * width + HEAD_DIM].astype(BF16)
        ref[0, hh, HEAD_DIM - off:2 * HEAD_DIM - off, :] = zeros


def _l0_in_kernel(x_ref, posr_ref, posc_ref, preg_ref, wn_ref, wt_ref, qag_ref, wqbt_ref,
                  kvag_ref, wk_ref, wvt_ref, invfr_ref, invfc_ref,
                  qtsb_ref, ksb_ref, vtsb_ref, gate_ref, qtmla_ref, kmla_ref, kxmla_ref, vtmla_ref):
    x = x_ref[0]
    h = _rms(x, preg_ref[...]).astype(BF16)

    def nat(a, b):
        return _dot(h, wn_ref[:, a:b])

    ksb_ref[0] = nat(0, 512).astype(BF16)
    gate_ref[0] = nat(512, 1536).astype(BF16)
    qa = nat(1536, 1920)
    ckv = nat(1920, 2176)
    kr = nat(2176, 2304)
    krr = nat(2304, 2432)

    tr = _dot_nt(wt_ref[...], h)
    _write_padded_heads(qtsb_ref, tr[0:SB_WIDTH], SB_HEADS, HEAD_DIM)
    vtsb_ref[0] = tr[SB_WIDTH:2 * SB_WIDTH].astype(BF16)

    qan = _rms(qa, qag_ref[...]).astype(BF16)
    qt = _dot_nt(wqbt_ref[...], qan)
    angt = invfc_ref[...] * posr_ref[0].astype(F32)
    cost = jnp.cos(angt)
    sint = jnp.sin(angt)
    tm = qt.shape[1]
    qtmla_ref[...] = jnp.zeros(qtmla_ref.shape, BF16)
    per_head = MLA_NOPE + MLA_ROPE
    half = MLA_ROPE // 2
    for hh in range(MLA_HEADS):
        base = hh * per_head
        off = MLA_NOPE * (hh % 2)
        qtmla_ref[0, hh, off:off + MLA_NOPE, :] = qt[base:base + MLA_NOPE].astype(BF16)
        x1 = qt[base + MLA_NOPE:base + MLA_NOPE + half]
        x2 = qt[base + MLA_NOPE + half:base + per_head]
        qtmla_ref[0, hh, LANES:LANES + half, :] = (x1 * cost - x2 * sint).astype(BF16)
        qtmla_ref[0, hh, LANES + half:LANES + MLA_ROPE, :] = (x2 * cost + x1 * sint).astype(BF16)

    ckvn = _rms(ckv, kvag_ref[...]).astype(BF16)
    kmla_ref[0] = _dot(ckvn, wk_ref[...]).astype(BF16)
    vtmla_ref[0] = _dot_nt(wvt_ref[...], ckvn).astype(BF16)
    ang = posc_ref[0].astype(F32) * invfr_ref[...]
    kxmla_ref[0] = (kr * jnp.cos(ang) + krr * jnp.sin(ang)).astype(BF16)


def _l0_in(x, positions, pre_g, wn, wt, qag, wqbt, kvag, wk, wvt, invf, tm):
    b, s, d = x.shape
    nt = s // tm
    posr = positions.reshape(b, 1, s)
    posc = positions.reshape(b, s, 1)
    invfr = jnp.tile(invf, LANES // invf.shape[0]).reshape(1, LANES)
    invfc = invf.reshape(-1, 1)
    const = lambda shape: pl.BlockSpec(shape, lambda bi, ti: (0,) * len(shape))
    row = lambda w: pl.BlockSpec((1, tm, w), lambda bi, ti: (bi, ti, 0))
    colt = lambda r: pl.BlockSpec((1, r, tm), lambda bi, ti: (bi, 0, ti))
    headt = lambda hn, r: pl.BlockSpec((1, hn, r, tm), lambda bi, ti: (bi, 0, 0, ti))
    out_shape = (
        jax.ShapeDtypeStruct((b, SB_HEADS, LANES, s), BF16),
        jax.ShapeDtypeStruct((b, s, SB_WIDTH), BF16),
        jax.ShapeDtypeStruct((b, SB_WIDTH, s), BF16),
        jax.ShapeDtypeStruct((b, s, SB_WIDTH + MLA_WIDTH), BF16),
        jax.ShapeDtypeStruct((b, MLA_HEADS, 2 * LANES, s), BF16),
        jax.ShapeDtypeStruct((b, s, MLA_WIDTH), BF16),
        jax.ShapeDtypeStruct((b, s, LANES), BF16),
        jax.ShapeDtypeStruct((b, MLA_WIDTH, s), BF16),
    )
    return pl.pallas_call(
        _l0_in_kernel,
        out_shape=out_shape,
        grid=(b, nt),
        in_specs=[
            row(d),
            pl.BlockSpec((1, 1, tm), lambda bi, ti: (bi, 0, ti)),
            pl.BlockSpec((1, tm, 1), lambda bi, ti: (bi, ti, 0)),
            const((1, d)), const(wn.shape), const(wt.shape), const((1, MLA_Q_LORA)), const(wqbt.shape),
            const((1, MLA_KV_LORA)), const(wk.shape), const(wvt.shape),
            const((1, LANES)), const((MLA_ROPE // 2, 1)),
        ],
        out_specs=(headt(SB_HEADS, LANES), row(SB_WIDTH), colt(SB_WIDTH), row(SB_WIDTH + MLA_WIDTH),
                   headt(MLA_HEADS, 2 * LANES), row(MLA_WIDTH), row(LANES), colt(MLA_WIDTH)),
        compiler_params=pltpu.CompilerParams(
            dimension_semantics=("arbitrary", "arbitrary"), vmem_limit_bytes=VMEM_LIMIT),
        name="l0_in",
    )(x, posr, posc, pre_g.reshape(1, d), wn, wt, qag.reshape(1, -1), wqbt,
      kvag.reshape(1, -1), wk, wvt, invfr, invfc)


def _l1_in_kernel(x_ref, preg_ref, wn_ref, wt_ref, bf_ref, tri_ref, place_ref,
                  qt_ref, k_ref, vt_ref, gate_ref, kx_ref, carry_ref):
    @pl.when(pl.program_id(1) == 0)
    def _():
        carry_ref[...] = jnp.zeros(carry_ref.shape, F32)

    x = x_ref[0]
    h = _rms(x, preg_ref[...]).astype(BF16)
    k_ref[0] = _dot(h, wn_ref[:, 0:1024]).astype(BF16)
    gate_ref[0] = _dot(h, wn_ref[:, 1024:2048]).astype(BF16)
    f = _dot(h, wn_ref[:, 2048:2176])[:, 0:FOX_HEADS] + bf_ref[...]

    tr = _dot_nt(wt_ref[...], h)
    _write_padded_heads(qt_ref, tr[0:FOX_WIDTH], FOX_HEADS, HEAD_DIM)
    vt_ref[0] = tr[FOX_WIDTH:2 * FOX_WIDTH].astype(BF16)

    logf = jnp.minimum(f, 0.0) - jnp.log1p(jnp.exp(-jnp.abs(f)))
    tri = tri_ref[...]
    c = carry_ref[0:1, 0:FOX_HEADS]
    for term in _split3(logf):
        c = c + _dot(tri, term)
    tm = c.shape[0]
    carry_ref[0:1, 0:FOX_HEADS] = c[tm - 1:tm, :]
    slab = jnp.zeros((tm, LANES), F32)
    for t, term in enumerate(_split3(-LOG2E * c)):
        slab = slab + _dot(term, place_ref[t])
    kx_ref[0] = slab.astype(BF16)


def _l1_in(x, pre_g, wn, wt, bf, tm):
    b, s, d = x.shape
    nt = s // tm
    tri = (lax.broadcasted_iota(jnp.int32, (tm, tm), 1)
           <= lax.broadcasted_iota(jnp.int32, (tm, tm), 0)).astype(BF16)
    hh = lax.broadcasted_iota(jnp.int32, (3, FOX_HEADS, LANES), 1)
    tt = lax.broadcasted_iota(jnp.int32, (3, FOX_HEADS, LANES), 0)
    ll = lax.broadcasted_iota(jnp.int32, (3, FOX_HEADS, LANES), 2)
    place = (ll == tt * FOX_HEADS + hh).astype(BF16)
    const = lambda shape: pl.BlockSpec(shape, lambda bi, ti: (0,) * len(shape))
    row = lambda w: pl.BlockSpec((1, tm, w), lambda bi, ti: (bi, ti, 0))
    out_shape = (
        jax.ShapeDtypeStruct((b, FOX_HEADS, LANES, s), BF16),
        jax.ShapeDtypeStruct((b, s, FOX_WIDTH), BF16),
        jax.ShapeDtypeStruct((b, FOX_WIDTH, s), BF16),
        jax.ShapeDtypeStruct((b, s, FOX_WIDTH), BF16),
        jax.ShapeDtypeStruct((b, s, LANES), BF16),
    )
    return pl.pallas_call(
        _l1_in_kernel,
        out_shape=out_shape,
        grid=(b, nt),
        in_specs=[row(d), const((1, d)), const(wn.shape), const(wt.shape),
                  const((1, FOX_HEADS)), const((tm, tm)), const(place.shape)],
        out_specs=(pl.BlockSpec((1, FOX_HEADS, LANES, tm), lambda bi, ti: (bi, 0, 0, ti)),
                   row(FOX_WIDTH),
                   pl.BlockSpec((1, FOX_WIDTH, tm), lambda bi, ti: (bi, 0, ti)),
                   row(FOX_WIDTH), row(LANES)),
        scratch_shapes=[pltpu.VMEM((8, LANES), F32)],
        compiler_params=pltpu.CompilerParams(
            dimension_semantics=("arbitrary", "arbitrary"), vmem_limit_bytes=VMEM_LIMIT),
        name="l1_in",
    )(x, pre_g.reshape(1, d), wn, wt, bf.reshape(1, -1), tri, place)


def _attn_kernel(*refs, mode, extra, tq, ch, dch, split):
    refs = list(refs)
    qt_ref = refs.pop(0)
    k_ref = refs.pop(0)
    kx_ref = refs.pop(0) if extra else None
    vt_ref = refs.pop(0)
    u_ref = refs.pop(0) if mode == "stick" else None
    o_ref, acc_ref, stat_ref, sbuf_ref = refs[:4]
    zl_ref = refs[4] if mode == "stick" else None

    pi = pl.program_id(1)
    q0 = pl.program_id(2) * tq
    nd = tq // dch

    qts = []
    for hh in range(2):
        qt = qt_ref[0, hh]
        if extra == "bias":
            hg = 2 * pi + hh
            r = lax.broadcasted_iota(jnp.int32, (LANES, tq), 0)
            hit = (r == hg) | (r == hg + FOX_HEADS) | (r == hg + 2 * FOX_HEADS)
            qt = jnp.concatenate([qt, jnp.where(hit, 1.0, 0.0).astype(BF16)], axis=0)
        qts.append(qt)

    def k_chunk(start, n):
        k = k_ref[0, pl.ds(start, n), :]
        if extra:
            k = jnp.concatenate([k, kx_ref[0, pl.ds(start, n), :]], axis=1)
        return k

    def v_chunk(hh, start, n):
        return vt_ref[0, hh * HEAD_DIM:(hh + 1) * HEAD_DIM, pl.ds(start, n)]

    acc_ref[...] = jnp.zeros(acc_ref.shape, F32)
    rr = lax.broadcasted_iota(jnp.int32, (dch, dch), 0)
    cc = lax.broadcasted_iota(jnp.int32, (dch, dch), 1)

    def diag_ranges(c):
        off = c * dch
        out = [(off, off + dch, True)]
        if off + dch < tq:
            out.append((off + dch, tq, False))
        return out

    if mode == "softmax":
        nfull = q0 // ch
        stat_ref[...] = jnp.where(lax.broadcasted_iota(jnp.int32, stat_ref.shape, 0) % 2 == 0,
                                  MASK_VALUE, 0.0)

        def update(hh, start, n, c0, c1, tri):
            def scores():
                s = sbuf_ref[hh, 0:n, c0:c1]
                return jnp.where(rr <= cc, s, MASK_VALUE) if tri else s

            m_prev = stat_ref[2 * hh:2 * hh + 1, c0:c1]
            l_prev = stat_ref[2 * hh + 1:2 * hh + 2, c0:c1]
            m_new = jnp.maximum(m_prev, jnp.max(scores(), axis=0, keepdims=True))
            alpha = jnp.exp2(m_prev - m_new)
            p = jnp.exp2(scores() - m_new)
            stat_ref[2 * hh:2 * hh + 1, c0:c1] = m_new
            stat_ref[2 * hh + 1:2 * hh + 2, c0:c1] = alpha * l_prev + jnp.sum(p, axis=0, keepdims=True)
            pv = _dot(v_chunk(hh, start, n), p.astype(BF16))
            acc_ref[hh, :, c0:c1] = alpha * acc_ref[hh, :, c0:c1] + pv

        k0 = k_chunk(0, ch)
        for hh in range(2):
            sbuf_ref[hh] = _dot(k0, qts[hh])

        def body(j, carry):
            start = pl.multiple_of(j * ch, ch)
            kn = k_chunk(pl.multiple_of(start + ch, ch), ch)
            for hh in range(2):
                s_next = _dot(kn, qts[hh])
                for c0 in range(0, tq, split):
                    update(hh, start, ch, c0, c0 + split, False)
                sbuf_ref[hh] = s_next
            return carry

        lax.fori_loop(0, nfull, body, 0)

        for c in range(nd):
            off = c * dch
            start = pl.multiple_of(q0 + off, dch)
            for hh in range(2):
                if c > 0:
                    sbuf_ref[hh, 0:dch, off:] = _dot(k_chunk(start, dch), qts[hh][:, off:])
                for c0, c1, tri in diag_ranges(c):
                    update(hh, start, dch, c0, c1, tri)
        o_t = jnp.concatenate([acc_ref[0] / stat_ref[1:2, :], acc_ref[1] / stat_ref[3:4, :]], axis=0)
    else:
        nfull = q0 // dch
        stat_ref[...] = jnp.zeros(stat_ref.shape, F32)

        def update(hh, start, n, c0, c1, tri):
            z = sbuf_ref[hh, 0:n, c0:c1]
            sp = jnp.maximum(z, 0.0) + jnp.log(1.0 + jnp.exp2(jnp.abs(z) * -LOG2E))
            if tri:
                sp = jnp.where(rr < cc, sp, 0.0)
            zl_ref[hh, 0:n, c0:c1] = z - sp
            hi = sp.astype(BF16)
            lo = (sp - hi.astype(F32)).astype(BF16)
            u = u_ref[...]
            remain = _dot(u, hi) + _dot(u, lo)
            carry = stat_ref[hh:hh + 1, c0:c1]
            stat_ref[hh:hh + 1, c0:c1] = carry + remain[0:1, :] - sp[0:1, :]
            w = jnp.exp(zl_ref[hh, 0:n, c0:c1] + remain + carry)
            if tri:
                w = jnp.where(rr < cc, w, 0.0)
            acc_ref[hh, :, c0:c1] = acc_ref[hh, :, c0:c1] + _dot(v_chunk(hh, start, n), w.astype(BF16))

        for c in reversed(range(nd)):
            off = c * dch
            start = pl.multiple_of(q0 + off, dch)
            kc = k_chunk(start, dch)
            for hh in range(2):
                sbuf_ref[hh, :, off:] = _dot(kc, qts[hh][:, off:])
                for c0, c1, tri in diag_ranges(c):
                    update(hh, start, dch, c0, c1, tri)

        kl = k_chunk(pl.multiple_of(jnp.maximum(nfull - 1, 0) * dch, dch), dch)
        for hh in range(2):
            sbuf_ref[hh] = _dot(kl, qts[hh])

        def body(t, carry):
            j = nfull - 1 - t
            start = pl.multiple_of(j * dch, dch)
            kn = k_chunk(pl.multiple_of(jnp.maximum(j - 1, 0) * dch, dch), dch)
            for hh in range(2):
                s_next = _dot(kn, qts[hh])
                for c0 in range(0, tq, split):
                    update(hh, start, dch, c0, c0 + split, False)
                sbuf_ref[hh] = s_next
            return carry

        lax.fori_loop(0, nfull, body, 0)
        o_t = jnp.concatenate([acc_ref[0], acc_ref[1]], axis=0)

    o_ref[0] = o_t.T.astype(o_ref.dtype)


def _attention(qt, k, vt, kx, *, mode, extra, tq, ch, dch):
    b, s, width = k.shape
    heads = qt.shape[1]
    kd = qt.shape[2]
    in_specs = [pl.BlockSpec((1, 2, kd, tq), lambda bi, pi, qi: (bi, pi, 0, qi)),
                pl.BlockSpec((1, s, LANES), lambda bi, pi, qi: (bi, 0, pi))]
    args = [qt, k]
    if extra:
        in_specs.append(pl.BlockSpec((1, s, LANES), lambda bi, pi, qi: (bi, 0, 0)))
        args.append(kx)
    in_specs.append(pl.BlockSpec((1, LANES, s), lambda bi, pi, qi: (bi, pi, 0)))
    args.append(vt)
    sch = dch if mode == "stick" else ch
    scratch = [pltpu.VMEM((2, HEAD_DIM, tq), F32),
               pltpu.VMEM((8, tq), F32),
               pltpu.VMEM((2, sch, tq), F32)]
    if mode == "stick":
        upper = jnp.where(lax.broadcasted_iota(jnp.int32, (dch, dch), 1)
                          > lax.broadcasted_iota(jnp.int32, (dch, dch), 0), -1.0, 0.0).astype(BF16)
        in_specs.append(pl.BlockSpec((dch, dch), lambda bi, pi, qi: (0, 0)))
        args.append(upper)
        scratch.append(pltpu.VMEM((2, dch, tq), F32))
    kernel = functools.partial(_attn_kernel, mode=mode, extra=extra, tq=tq, ch=ch, dch=dch,
                               split=min(COLUMN_SPLIT, tq))
    return pl.pallas_call(
        kernel,
        out_shape=jax.ShapeDtypeStruct((b, s, width), BF16),
        grid=(b, heads // 2, s // tq),
        in_specs=in_specs,
        out_specs=pl.BlockSpec((1, tq, LANES), lambda bi, pi, qi: (bi, qi, pi)),
        scratch_shapes=scratch,
        compiler_params=pltpu.CompilerParams(
            dimension_semantics=("arbitrary", "arbitrary", "arbitrary"), vmem_limit_bytes=VMEM_LIMIT),
        name="attn_" + mode + ("_" + extra if extra else ""),
    )(*args)


def _out_kernel(*refs, n_o):
    o_refs = refs[:n_o]
    gate_ref, x_ref, w_ref, g_ref, out_ref = refs[n_o:]
    gate = _silu(gate_ref[...].astype(F32))
    y = None
    col = 0
    for o_ref in o_refs:
        wdt = o_ref.shape[1]
        og = (o_ref[...].astype(F32) * gate[:, col:col + wdt]).astype(BF16)
        part = _dot(og, w_ref[col:col + wdt, :])
        y = part if y is None else y + part
        col += wdt
    out_ref[...] = x_ref[...] + _rms(y, g_ref[...])


def _out_stage(os, gate, x, w, g, tm):
    n, d = x.shape
    os = [o.reshape(n, o.shape[-1]) for o in os]
    in_specs = [pl.BlockSpec((tm, o.shape[1]), lambda i: (i, 0)) for o in os]
    in_specs += [pl.BlockSpec((tm, gate.shape[1]), lambda i: (i, 0)),
                 pl.BlockSpec((tm, d), lambda i: (i, 0)),
                 pl.BlockSpec(w.shape, lambda i: (0, 0)),
                 pl.BlockSpec((1, d), lambda i: (0, 0))]
    return pl.pallas_call(
        functools.partial(_out_kernel, n_o=len(os)),
        out_shape=jax.ShapeDtypeStruct((n, d), F32),
        grid=(n // tm,),
        in_specs=in_specs,
        out_specs=pl.BlockSpec((tm, d), lambda i: (i, 0)),
        compiler_params=pltpu.CompilerParams(
            dimension_semantics=("arbitrary",), vmem_limit_bytes=VMEM_LIMIT),
        name="out_stage",
    )(*os, gate, x, w, g.reshape(1, d))


def kernel(x, positions, l0_pre_g, l0_post_g, l0_w_in, l0_q_a_g, l0_w_q_b, l0_kv_a_g, l0_w_kv_b, l0_w_out,
           l1_pre_g, l1_post_g, l1_w_in, l1_b_f, l1_w_out):
    b, s, d = x.shape
    tm = min(ROW_TILE, s)
    tq = min(Q_TILE, s)
    dch = min(DIAG_CHUNK, tq)
    ch = min(SOFTMAX_CHUNK, tq)
    sb_scale = HEAD_DIM ** -0.5
    mla_scale = (MLA_NOPE + MLA_ROPE) ** -0.5

    w = l0_w_in
    c0 = 4 * SB_WIDTH
    c1 = c0 + MLA_Q_LORA
    c2 = c1 + MLA_KV_LORA
    c3 = c2 + MLA_ROPE
    half = MLA_ROPE // 2
    wkr = w[:, c2:c3]
    wkr_rot = jnp.concatenate([-wkr[:, half:], wkr[:, :half]], axis=1)
    lane_pad = jnp.zeros((d, LANES - MLA_ROPE), F32)
    wn0 = jnp.concatenate([w[:, 512:1024], w[:, 1536:2048], w[:, c3:], w[:, c0:c1], w[:, c1:c2],
                           wkr, lane_pad, wkr_rot, lane_pad], axis=1).astype(BF16)
    wt0 = jnp.concatenate([w[:, 0:512] * sb_scale, w[:, 1024:1536]], axis=1).T.astype(BF16)
    wqbt = (l0_w_q_b * (mla_scale * LOG2E)).T.astype(BF16)
    wkv = l0_w_kv_b.reshape(MLA_KV_LORA, MLA_HEADS, MLA_NOPE + MLA_V)
    wk = wkv[:, :, :MLA_NOPE].reshape(MLA_KV_LORA, -1).astype(BF16)
    wvt = wkv[:, :, MLA_NOPE:].reshape(MLA_KV_LORA, -1).T.astype(BF16)
    invf = ROPE_THETA ** (-jnp.arange(0, MLA_ROPE, 2, dtype=F32) / MLA_ROPE)

    qt_sb, k_sb, vt_sb, gate0, qt_mla, k_mla, kx_mla, vt_mla = _l0_in(
        x, positions, l0_pre_g, wn0, wt0, l0_q_a_g, wqbt, l0_kv_a_g, wk, wvt, invf, tm)
    o_sb = _attention(qt_sb, k_sb, vt_sb, None, mode="stick", extra=None, tq=tq, ch=ch, dch=dch)
    o_mla = _attention(qt_mla, k_mla, vt_mla, kx_mla, mode="softmax", extra="rows", tq=tq, ch=ch, dch=dch)
    x1 = _out_stage([o_sb, o_mla], gate0.reshape(b * s, -1), x.reshape(b * s, d),
                    l0_w_out.astype(BF16), l0_post_g, tm)

    w1 = l1_w_in
    f_pad = jnp.zeros((d, LANES - FOX_HEADS), F32)
    wn1 = jnp.concatenate([w1[:, 1024:2048], w1[:, 3072:4096], w1[:, 4096:], f_pad], axis=1).astype(BF16)
    wt1 = jnp.concatenate([w1[:, 0:1024] * (sb_scale * LOG2E), w1[:, 2048:3072]], axis=1).T.astype(BF16)
    qt1, k1, vt1, gate1, kx1 = _l1_in(x1.reshape(b, s, d), l1_pre_g, wn1, wt1, l1_b_f, tm)
    o1 = _attention(qt1, k1, vt1, kx1, mode="softmax", extra="bias", tq=tq, ch=ch, dch=dch)
    out = _out_stage([o1], gate1.reshape(b * s, -1), x1, l1_w_out.astype(BF16), l1_post_g, tm)
    return out.reshape(b, s, d)
```

```python
import functools
import math

import jax
import jax.numpy as jnp
from jax import lax
from jax.experimental import pallas as pl
from jax.experimental.pallas import tpu as pltpu

D_MODEL = 1024
RMS_EPS = 1e-6
HEAD_DIM = 64
SB_HEADS = 8
SB_WIDTH = SB_HEADS * HEAD_DIM
MLA_HEADS = 8
MLA_NOPE = 64
MLA_ROPE = 32
MLA_V = 64
MLA_Q_LORA = 384
MLA_KV_LORA = 256
MLA_WIDTH = MLA_HEADS * MLA_V
ROPE_THETA = 10000.0
FOX_HEADS = 16
FOX_WIDTH = FOX_HEADS * HEAD_DIM

LANES = 128
ROW_TILE = 256
Q_TILE = 1024
SOFTMAX_CHUNK = 512
DIAG_CHUNK = 256
MASK_VALUE = -1e30
LOG2E = math.log2(math.e)
VMEM_LIMIT = 48 * 1024 * 1024
F32 = jnp.float32
BF16 = jnp.bfloat16


def _rms(xf, g):
    var = jnp.mean(xf * xf, axis=-1, keepdims=True)
    return xf * lax.rsqrt(var + RMS_EPS) * g


def _dot(a, b):
    return jnp.dot(a, b, preferred_element_type=F32)


def _dot_nt(a, b):
    return lax.dot_general(a, b, (((1,), (1,)), ((), ())), preferred_element_type=F32)


def _silu(g):
    return g / (1.0 + jnp.exp(-g))


def _split3(v):
    a1 = v.astype(BF16)
    r1 = v - a1.astype(F32)
    a2 = r1.astype(BF16)
    a3 = (r1 - a2.astype(F32)).astype(BF16)
    return a1, a2, a3


def _write_padded_heads(ref, rows_t, heads, width):
    tm = rows_t.shape[1]
    zeros = jnp.zeros((HEAD_DIM, tm), BF16)
    for hh in range(heads):
        off = HEAD_DIM * (hh % 2)
        ref[0, hh, off:off + HEAD_DIM, :] = rows_t[hh * width:hh * width + HEAD_DIM].astype(BF16)
        ref[0, hh, HEAD_DIM - off:2 * HEAD_DIM - off, :] = zeros


def _l0_in_kernel(x_ref, posr_ref, posc_ref, preg_ref, wn_ref, wt_ref, qag_ref, wqbt_ref,
                  kvag_ref, wk_ref, wvt_ref, invfr_ref, invfc_ref,
                  qtsb_ref, ksb_ref, vtsb_ref, gate_ref, qtmla_ref, kmla_ref, kxmla_ref, vtmla_ref):
    x = x_ref[0]
    h = _rms(x, preg_ref[...]).astype(BF16)

    def nat(a, b):
        return _dot(h, wn_ref[:, a:b])

    ksb_ref[0] = nat(0, 512).astype(BF16)
    gate_ref[0] = nat(512, 1536).astype(BF16)
    qa = nat(1536, 1920)
    ckv = nat(1920, 2176)
    kr = nat(2176, 2304)
    krr = nat(2304, 2432)

    tr = _dot_nt(wt_ref[...], h)
    _write_padded_heads(qtsb_ref, tr[0:SB_WIDTH], SB_HEADS, HEAD_DIM)
    vtsb_ref[0] = tr[SB_WIDTH:2 * SB_WIDTH].astype(BF16)

    qan = _rms(qa, qag_ref[...]).astype(BF16)
    qt = _dot_nt(wqbt_ref[...], qan)
    angt = invfc_ref[...] * posr_ref[0].astype(F32)
    cost = jnp.cos(angt)
    sint = jnp.sin(angt)
    tm = qt.shape[1]
    qtmla_ref[...] = jnp.zeros(qtmla_ref.shape, BF16)
    per_head = MLA_NOPE + MLA_ROPE
    half = MLA_ROPE // 2
    for hh in range(MLA_HEADS):
        base = hh * per_head
        off = MLA_NOPE * (hh % 2)
        qtmla_ref[0, hh, off:off + MLA_NOPE, :] = qt[base:base + MLA_NOPE].astype(BF16)
        x1 = qt[base + MLA_NOPE:base + MLA_NOPE + half]
        x2 = qt[base + MLA_NOPE + half:base + per_head]
        qtmla_ref[0, hh, LANES:LANES + half, :] = (x1 * cost - x2 * sint).astype(BF16)
        qtmla_ref[0, hh, LANES + half:LANES + MLA_ROPE, :] = (x2 * cost + x1 * sint).astype(BF16)

    ckvn = _rms(ckv, kvag_ref[...]).astype(BF16)
    kmla_ref[0] = _dot(ckvn, wk_ref[...]).astype(BF16)
    vtmla_ref[0] = _dot_nt(wvt_ref[...], ckvn).astype(BF16)
    ang = posc_ref[0].astype(F32) * invfr_ref[...]
    kxmla_ref[0] = (kr * jnp.cos(ang) + krr * jnp.sin(ang)).astype(BF16)


def _l0_in(x, positions, pre_g, wn, wt, qag, wqbt, kvag, wk, wvt, invf, tm):
    b, s, d = x.shape
    nt = s // tm
    posr = positions.reshape(b, 1, s)
    posc = positions.reshape(b, s, 1)
    invfr = jnp.tile(invf, LANES // invf.shape[0]).reshape(1, LANES)
    invfc = invf.reshape(-1, 1)
    const = lambda shape: pl.BlockSpec(shape, lambda bi, ti: (0,) * len(shape))
    row = lambda w: pl.BlockSpec((1, tm, w), lambda bi, ti: (bi, ti, 0))
    colt = lambda r: pl.BlockSpec((1, r, tm), lambda bi, ti: (bi, 0, ti))
    headt = lambda hn, r: pl.BlockSpec((1, hn, r, tm), lambda bi, ti: (bi, 0, 0, ti))
    out_shape = (
        jax.ShapeDtypeStruct((b, SB_HEADS, LANES, s), BF16),
        jax.ShapeDtypeStruct((b, s, SB_WIDTH), BF16),
        jax.ShapeDtypeStruct((b, SB_WIDTH, s), BF16),
        jax.ShapeDtypeStruct((b, s, SB_WIDTH + MLA_WIDTH), BF16),
        jax.ShapeDtypeStruct((b, MLA_HEADS, 2 * LANES, s), BF16),
        jax.ShapeDtypeStruct((b, s, MLA_WIDTH), BF16),
        jax.ShapeDtypeStruct((b, s, LANES), BF16),
        jax.ShapeDtypeStruct((b, MLA_WIDTH, s), BF16),
    )
    return pl.pallas_call(
        _l0_in_kernel,
        out_shape=out_shape,
        grid=(b, nt),
        in_specs=[
            row(d),
            pl.BlockSpec((1, 1, tm), lambda bi, ti: (bi, 0, ti)),
            pl.BlockSpec((1, tm, 1), lambda bi, ti: (bi, ti, 0)),
            const((1, d)), const(wn.shape), const(wt.shape), const((1, MLA_Q_LORA)), const(wqbt.shape),
            const((1, MLA_KV_LORA)), const(wk.shape), const(wvt.shape),
            const((1, LANES)), const((MLA_ROPE // 2, 1)),
        ],
        out_specs=(headt(SB_HEADS, LANES), row(SB_WIDTH), colt(SB_WIDTH), row(SB_WIDTH + MLA_WIDTH),
                   headt(MLA_HEADS, 2 * LANES), row(MLA_WIDTH), row(LANES), colt(MLA_WIDTH)),
        compiler_params=pltpu.CompilerParams(
            dimension_semantics=("arbitrary", "arbitrary"), vmem_limit_bytes=VMEM_LIMIT),
        name="l0_in",
    )(x, posr, posc, pre_g.reshape(1, d), wn, wt, qag.reshape(1, -1), wqbt,
      kvag.reshape(1, -1), wk, wvt, invfr, invfc)


def _l1_in_kernel(x_ref, preg_ref, wn_ref, wt_ref, bf_ref, tri_ref, place_ref,
                  qt_ref, k_ref, vt_ref, gate_ref, kx_ref, carry_ref):
    @pl.when(pl.program_id(1) == 0)
    def _():
        carry_ref[...] = jnp.zeros(carry_ref.shape, F32)

    x = x_ref[0]
    h = _rms(x, preg_ref[...]).astype(BF16)
    k_ref[0] = _dot(h, wn_ref[:, 0:1024]).astype(BF16)
    gate_ref[0] = _dot(h, wn_ref[:, 1024:2048]).astype(BF16)
    f = _dot(h, wn_ref[:, 2048:2176])[:, 0:FOX_HEADS] + bf_ref[...]

    tr = _dot_nt(wt_ref[...], h)
    _write_padded_heads(qt_ref, tr[0:FOX_WIDTH], FOX_HEADS, HEAD_DIM)
    vt_ref[0] = tr[FOX_WIDTH:2 * FOX_WIDTH].astype(BF16)

    logf = jnp.minimum(f, 0.0) - jnp.log1p(jnp.exp(-jnp.abs(f)))
    tri = tri_ref[...]
    c = carry_ref[0:1, 0:FOX_HEADS]
    for term in _split3(logf):
        c = c + _dot(tri, term)
    tm = c.shape[0]
    carry_ref[0:1, 0:FOX_HEADS] = c[tm - 1:tm, :]
    slab = jnp.zeros((tm, LANES), F32)
    for t, term in enumerate(_split3(-LOG2E * c)):
        slab = slab + _dot(term, place_ref[t])
    kx_ref[0] = slab.astype(BF16)


def _l1_in(x, pre_g, wn, wt, bf, tm):
    b, s, d = x.shape
    nt = s // tm
    tri = (lax.broadcasted_iota(jnp.int32, (tm, tm), 1)
           <= lax.broadcasted_iota(jnp.int32, (tm, tm), 0)).astype(BF16)
    hh = lax.broadcasted_iota(jnp.int32, (3, FOX_HEADS, LANES), 1)
    tt = lax.broadcasted_iota(jnp.int32, (3, FOX_HEADS, LANES), 0)
    ll = lax.broadcasted_iota(jnp.int32, (3, FOX_HEADS, LANES), 2)
    place = (ll == tt * FOX_HEADS + hh).astype(BF16)
    const = lambda shape: pl.BlockSpec(shape, lambda bi, ti: (0,) * len(shape))
    row = lambda w: pl.BlockSpec((1, tm, w), lambda bi, ti: (bi, ti, 0))
    out_shape = (
        jax.ShapeDtypeStruct((b, FOX_HEADS, LANES, s), BF16),
        jax.ShapeDtypeStruct((b, s, FOX_WIDTH), BF16),
        jax.ShapeDtypeStruct((b, FOX_WIDTH, s), BF16),
        jax.ShapeDtypeStruct((b, s, FOX_WIDTH), BF16),
        jax.ShapeDtypeStruct((b, s, LANES), BF16),
    )
    return pl.pallas_call(
        _l1_in_kernel,
        out_shape=out_shape,
        grid=(b, nt),
        in_specs=[row(d), const((1, d)), const(wn.shape), const(wt.shape),
                  const((1, FOX_HEADS)), const((tm, tm)), const(place.shape)],
        out_specs=(pl.BlockSpec((1, FOX_HEADS, LANES, tm), lambda bi, ti: (bi, 0, 0, ti)),
                   row(FOX_WIDTH),
                   pl.BlockSpec((1, FOX_WIDTH, tm), lambda bi, ti: (bi, 0, ti)),
                   row(FOX_WIDTH), row(LANES)),
        scratch_shapes=[pltpu.VMEM((8, LANES), F32)],
        compiler_params=pltpu.CompilerParams(
            dimension_semantics=("arbitrary", "arbitrary"), vmem_limit_bytes=VMEM_LIMIT),
        name="l1_in",
    )(x, pre_g.reshape(1, d), wn, wt, bf.reshape(1, -1), tri, place)


def _attn_kernel(*refs, mode, extra, tq, ch, dch):
    refs = list(refs)
    qt_ref = refs.pop(0)
    k_ref = refs.pop(0)
    kx_ref = refs.pop(0) if extra else None
    vt_ref = refs.pop(0)
    u_ref = refs.pop(0) if mode == "stick" else None
    o_ref, acc_ref, stat_ref, sbuf_ref = refs[:4]

    pi = pl.program_id(1)
    q0 = pl.program_id(2) * tq
    nd = tq // dch

    qts = []
    for hh in range(2):
        qt = qt_ref[0, hh]
        if extra == "bias":
            hg = 2 * pi + hh
            r = lax.broadcasted_iota(jnp.int32, (LANES, tq), 0)
            hit = (r == hg) | (r == hg + FOX_HEADS) | (r == hg + 2 * FOX_HEADS)
            qt = jnp.concatenate([qt, jnp.where(hit, 1.0, 0.0).astype(BF16)], axis=0)
        qts.append(qt)

    def k_chunk(start, n):
        k = k_ref[0, pl.ds(start, n), :]
        if extra:
            k = jnp.concatenate([k, kx_ref[0, pl.ds(start, n), :]], axis=1)
        return k

    def v_chunk(hh, start, n):
        return vt_ref[0, hh * HEAD_DIM:(hh + 1) * HEAD_DIM, pl.ds(start, n)]

    acc_ref[...] = jnp.zeros(acc_ref.shape, F32)
    rr = lax.broadcasted_iota(jnp.int32, (dch, dch), 0)
    cc = lax.broadcasted_iota(jnp.int32, (dch, dch), 1)

    def diag_ranges(c):
        off = c * dch
        out = [(off, off + dch, True)]
        if off + dch < tq:
            out.append((off + dch, tq, False))
        return out

    if mode == "softmax":
        nfull = q0 // ch
        srow = lax.broadcasted_iota(jnp.int32, stat_ref.shape, 0)
        stat_ref[...] = jnp.where(srow % 4 == 0, MASK_VALUE, 0.0)

        def rescale(hh, cmax, c0, c1):
            m_prev = stat_ref[4 * hh:4 * hh + 1, c0:c1]
            m_new = jnp.maximum(m_prev, cmax)
            stat_ref[4 * hh:4 * hh + 1, c0:c1] = m_new
            return m_new, jnp.exp2(m_prev - m_new)

        def accumulate(hh, p, alpha, start, n, c0, c1):
            l_prev = stat_ref[4 * hh + 1:4 * hh + 2, c0:c1]
            stat_ref[4 * hh + 1:4 * hh + 2, c0:c1] = alpha * l_prev + jnp.sum(p, axis=0, keepdims=True)
            pv = _dot(v_chunk(hh, start, n), p.astype(BF16))
            acc_ref[hh, :, c0:c1] = alpha * acc_ref[hh, :, c0:c1] + pv

        def produce(hh, slot, k):
            s = _dot(k, qts[hh])
            sbuf_ref[slot, hh] = s
            stat_ref[4 * hh + 2 + slot:4 * hh + 3 + slot, :] = jnp.max(s, axis=0, keepdims=True)

        k0 = k_chunk(0, ch)
        for hh in range(2):
            produce(hh, 0, k0)

        def half_step(j, slot):
            start = pl.multiple_of(j * ch, ch)
            kn = k_chunk(pl.multiple_of(start + ch, ch), ch)
            for hh in range(2):
                produce(hh, 1 - slot, kn)
                m_new, alpha = rescale(hh, stat_ref[4 * hh + 2 + slot:4 * hh + 3 + slot, :], 0, tq)
                accumulate(hh, jnp.exp2(sbuf_ref[slot, hh] - m_new), alpha, start, ch, 0, tq)

        def body(i, carry):
            half_step(2 * i, 0)
            half_step(2 * i + 1, 1)
            return carry

        lax.fori_loop(0, nfull // 2, body, 0)

        def stats(hh, slot, rb, n, c0, c1, tri):
            def scores():
                s = sbuf_ref[slot, hh, rb:rb + n, c0:c1]
                return jnp.where(rr <= cc, s, MASK_VALUE) if tri else s

            m_new, alpha = rescale(hh, jnp.max(scores(), axis=0, keepdims=True), c0, c1)
            return alpha, jnp.exp2(scores() - m_new)

        per = ch // dch
        for g in range(1, tq // ch):
            kb = k_chunk(pl.multiple_of(q0 + g * ch, ch), ch)
            for hh in range(2):
                sbuf_ref[g % 2, hh, :, g * ch:] = _dot(kb, qts[hh][:, g * ch:])
        for c in range(nd):
            g, r = divmod(c, per)
            gslot = g % 2
            start = pl.multiple_of(q0 + c * dch, dch)
            for hh in range(2):
                for c0, c1, tri in diag_ranges(c):
                    alpha, p = stats(hh, gslot, r * dch, dch, c0, c1, tri)
                    accumulate(hh, p, alpha, start, dch, c0, c1)
        o_t = jnp.concatenate([acc_ref[0] / stat_ref[1:2, :], acc_ref[1] / stat_ref[5:6, :]], axis=0)
    else:
        hl_ref = refs[4]
        nfull = q0 // dch
        stat_ref[...] = jnp.zeros(stat_ref.shape, F32)

        def mask_first(x, fill):
            first = jnp.where(rr < cc, x[:, 0:dch], fill)
            return first if x.shape[1] == dch else jnp.concatenate([first, x[:, dch:]], axis=1)

        def produce(hh, slot, k, off, diag):
            z = _dot(k, qts[hh][:, off:])
            sp = jnp.maximum(z, 0.0) + jnp.log(1.0 + jnp.exp2(jnp.abs(z) * -LOG2E))
            if diag:
                sp = mask_first(sp, 0.0)
            sbuf_ref[slot, hh, :, off:] = z - sp
            hl_ref[slot, hh, :, off:] = sp.astype(BF16)
            stat_ref[2 + 2 * slot + hh:3 + 2 * slot + hh, off:] = sp[0:1, :]

        def suffix_sums(hh, slot, off):
            u = u_ref[...]
            return _dot(u, hl_ref[slot, hh, :, off:])

        def consume(hh, slot, remain, start, off, diag):
            carry = stat_ref[hh:hh + 1, off:]
            first = stat_ref[2 + 2 * slot + hh:3 + 2 * slot + hh, off:]
            stat_ref[hh:hh + 1, off:] = carry + remain[0:1, :] - first
            w = jnp.exp(sbuf_ref[slot, hh, :, off:] + remain + carry)
            if diag:
                w = mask_first(w, 0.0)
            acc_ref[hh, :, off:] = acc_ref[hh, :, off:] + _dot(v_chunk(hh, start, dch), w.astype(BF16))

        for c in reversed(range(nd)):
            off = c * dch
            start = pl.multiple_of(q0 + off, dch)
            kc = k_chunk(start, dch)
            for hh in range(2):
                produce(hh, c % 2, kc, off, True)
                consume(hh, c % 2, suffix_sums(hh, c % 2, off), start, off, True)

        kl = k_chunk(pl.multiple_of(jnp.maximum(nfull - 1, 0) * dch, dch), dch)
        for hh in range(2):
            produce(hh, 1, kl, 0, False)

        def half_step(j, slot):
            start = pl.multiple_of(j * dch, dch)
            kn = k_chunk(pl.multiple_of(jnp.maximum(j - 1, 0) * dch, dch), dch)
            for hh in range(2):
                remain = suffix_sums(hh, slot, 0)
                produce(hh, 1 - slot, kn, 0, False)
                consume(hh, slot, remain, start, 0, False)

        def body(t, carry):
            j = nfull - 1 - 2 * t
            half_step(j, 1)
            half_step(j - 1, 0)
            return carry

        lax.fori_loop(0, nfull // 2, body, 0)
        o_t = jnp.concatenate([acc_ref[0], acc_ref[1]], axis=0)

    o_ref[0] = o_t.T.astype(o_ref.dtype)


def _attention(qt, k, vt, kx, *, mode, extra, tq, ch, dch):
    b, s, width = k.shape
    heads = qt.shape[1]
    kd = qt.shape[2]
    in_specs = [pl.BlockSpec((1, 2, kd, tq), lambda bi, pi, qi: (bi, pi, 0, qi)),
                pl.BlockSpec((1, s, LANES), lambda bi, pi, qi: (bi, 0, pi))]
    args = [qt, k]
    if extra:
        in_specs.append(pl.BlockSpec((1, s, LANES), lambda bi, pi, qi: (bi, 0, 0)))
        args.append(kx)
    in_specs.append(pl.BlockSpec((1, LANES, s), lambda bi, pi, qi: (bi, pi, 0)))
    args.append(vt)
    scratch = [pltpu.VMEM((2, HEAD_DIM, tq), F32), pltpu.VMEM((8, tq), F32)]
    if mode == "softmax":
        scratch.append(pltpu.VMEM((2, 2, ch, tq), F32))
    else:
        upper = jnp.where(lax.broadcasted_iota(jnp.int32, (dch, dch), 1)
                          > lax.broadcasted_iota(jnp.int32, (dch, dch), 0), -1.0, 0.0).astype(BF16)
        in_specs.append(pl.BlockSpec((dch, dch), lambda bi, pi, qi: (0, 0)))
        args.append(upper)
        scratch += [pltpu.VMEM((2, 2, dch, tq), F32),
                    pltpu.VMEM((2, 2, dch, tq), BF16)]
    kernel = functools.partial(_attn_kernel, mode=mode, extra=extra, tq=tq, ch=ch, dch=dch)
    return pl.pallas_call(
        kernel,
        out_shape=jax.ShapeDtypeStruct((b, s, width), BF16),
        grid=(b, heads // 2, s // tq),
        in_specs=in_specs,
        out_specs=pl.BlockSpec((1, tq, LANES), lambda bi, pi, qi: (bi, qi, pi)),
        scratch_shapes=scratch,
        compiler_params=pltpu.CompilerParams(
            dimension_semantics=("arbitrary", "arbitrary", "arbitrary"), vmem_limit_bytes=VMEM_LIMIT),
        name="attn_" + mode + ("_" + extra if extra else ""),
    )(*args)


def _out_kernel(*refs, n_o):
    o_refs = refs[:n_o]
    gate_ref, x_ref, w_ref, g_ref, out_ref = refs[n_o:]
    gate = _silu(gate_ref[...].astype(F32))
    y = None
    col = 0
    for o_ref in o_refs:
        wdt = o_ref.shape[1]
        og = (o_ref[...].astype(F32) * gate[:, col:col + wdt]).astype(BF16)
        part = _dot(og, w_ref[col:col + wdt, :])
        y = part if y is None else y + part
        col += wdt
    out_ref[...] = x_ref[...] + _rms(y, g_ref[...])


def _out_stage(os, gate, x, w, g, tm):
    n, d = x.shape
    os = [o.reshape(n, o.shape[-1]) for o in os]
    in_specs = [pl.BlockSpec((tm, o.shape[1]), lambda i: (i, 0)) for o in os]
    in_specs += [pl.BlockSpec((tm, gate.shape[1]), lambda i: (i, 0)),
                 pl.BlockSpec((tm, d), lambda i: (i, 0)),
                 pl.BlockSpec(w.shape, lambda i: (0, 0)),
                 pl.BlockSpec((1, d), lambda i: (0, 0))]
    return pl.pallas_call(
        functools.partial(_out_kernel, n_o=len(os)),
        out_shape=jax.ShapeDtypeStruct((n, d), F32),
        grid=(n // tm,),
        in_specs=in_specs,
        out_specs=pl.BlockSpec((tm, d), lambda i: (i, 0)),
        compiler_params=pltpu.CompilerParams(
            dimension_semantics=("arbitrary",), vmem_limit_bytes=VMEM_LIMIT),
        name="out_stage",
    )(*os, gate, x, w, g.reshape(1, d))


def kernel(x, positions, l0_pre_g, l0_post_g, l0_w_in, l0_q_a_g, l0_w_q_b, l0_kv_a_g, l0_w_kv_b, l0_w_out,
           l1_pre_g, l1_post_g, l1_w_in, l1_b_f, l1_w_out):
    b, s, d = x.shape
    tm = min(ROW_TILE, s)
    tq = min(Q_TILE, s)
    dch = min(DIAG_CHUNK, tq // 2)
    ch = min(SOFTMAX_CHUNK, tq // 2)
    assert tq % (2 * ch) == 0 and ch % dch == 0 and s % tq == 0
    sb_scale = HEAD_DIM ** -0.5
    mla_scale = (MLA_NOPE + MLA_ROPE) ** -0.5

    w = l0_w_in
    c0 = 4 * SB_WIDTH
    c1 = c0 + MLA_Q_LORA
    c2 = c1 + MLA_KV_LORA
    c3 = c2 + MLA_ROPE
    half = MLA_ROPE // 2
    wkr = w[:, c2:c3]
    wkr_rot = jnp.concatenate([-wkr[:, half:], wkr[:, :half]], axis=1)
    lane_pad = jnp.zeros((d, LANES - MLA_ROPE), F32)
    wn0 = jnp.concatenate([w[:, 512:1024], w[:, 1536:2048], w[:, c3:], w[:, c0:c1], w[:, c1:c2],
                           wkr, lane_pad, wkr_rot, lane_pad], axis=1).astype(BF16)
    wt0 = jnp.concatenate([w[:, 0:512] * sb_scale, w[:, 1024:1536]], axis=1).T.astype(BF16)
    wqbt = (l0_w_q_b * (mla_scale * LOG2E)).T.astype(BF16)
    wkv = l0_w_kv_b.reshape(MLA_KV_LORA, MLA_HEADS, MLA_NOPE + MLA_V)
    wk = wkv[:, :, :MLA_NOPE].reshape(MLA_KV_LORA, -1).astype(BF16)
    wvt = wkv[:, :, MLA_NOPE:].reshape(MLA_KV_LORA, -1).T.astype(BF16)
    invf = ROPE_THETA ** (-jnp.arange(0, MLA_ROPE, 2, dtype=F32) / MLA_ROPE)

    qt_sb, k_sb, vt_sb, gate0, qt_mla, k_mla, kx_mla, vt_mla = _l0_in(
        x, positions, l0_pre_g, wn0, wt0, l0_q_a_g, wqbt, l0_kv_a_g, wk, wvt, invf, tm)
    o_sb = _attention(qt_sb, k_sb, vt_sb, None, mode="stick", extra=None, tq=tq, ch=ch, dch=dch)
    o_mla = _attention(qt_mla, k_mla, vt_mla, kx_mla, mode="softmax", extra="rows", tq=tq, ch=ch, dch=dch)
    x1 = _out_stage([o_sb, o_mla], gate0.reshape(b * s, -1), x.reshape(b * s, d),
                    l0_w_out.astype(BF16), l0_post_g, tm)

    w1 = l1_w_in
    f_pad = jnp.zeros((d, LANES - FOX_HEADS), F32)
    wn1 = jnp.concatenate([w1[:, 1024:2048], w1[:, 3072:4096], w1[:, 4096:], f_pad], axis=1).astype(BF16)
    wt1 = jnp.concatenate([w1[:, 0:1024] * (sb_scale * LOG2E), w1[:, 2048:3072]], axis=1).T.astype(BF16)
    qt1, k1, vt1, gate1, kx1 = _l1_in(x1.reshape(b, s, d), l1_pre_g, wn1, wt1, l1_b_f, tm)
    o1 = _attention(qt1, k1, vt1, kx1, mode="softmax", extra="bias", tq=tq, ch=ch, dch=dch)
    out = _out_stage([o1], gate1.reshape(b * s, -1), x1, l1_w_out.astype(BF16), l1_post_g, tm)
    return out.reshape(b, s, d)
```

```python
import functools
import math

import jax
import jax.numpy as jnp
from jax import lax
from jax.experimental import pallas as pl
from jax.experimental.pallas import tpu as pltpu

D_MODEL = 1024
RMS_EPS = 1e-6
HEAD_DIM = 64
SB_HEADS = 8
SB_WIDTH = SB_HEADS * HEAD_DIM
MLA_HEADS = 8
MLA_NOPE = 64
MLA_ROPE = 32
MLA_V = 64
MLA_Q_LORA = 384
MLA_KV_LORA = 256
MLA_WIDTH = MLA_HEADS * MLA_V
ROPE_THETA = 10000.0
FOX_HEADS = 16
FOX_WIDTH = FOX_HEADS * HEAD_DIM

LANES = 128
ROW_TILE = 256
Q_TILE = 1024
SOFTMAX_CHUNK = 512
DIAG_CHUNK = 256
SUM_ROWS = 16
MASK_VALUE = -1e30
LOG2E = math.log2(math.e)
VMEM_LIMIT = 48 * 1024 * 1024
F32 = jnp.float32
BF16 = jnp.bfloat16


def _rms(xf, g):
    var = jnp.mean(xf * xf, axis=-1, keepdims=True)
    return xf * lax.rsqrt(var + RMS_EPS) * g


def _dot(a, b):
    return jnp.dot(a, b, preferred_element_type=F32)


def _dot_nt(a, b):
    return lax.dot_general(a, b, (((1,), (1,)), ((), ())), preferred_element_type=F32)


def _silu(g):
    return g / (1.0 + jnp.exp(-g))


def _split3(v):
    a1 = v.astype(BF16)
    r1 = v - a1.astype(F32)
    a2 = r1.astype(BF16)
    a3 = (r1 - a2.astype(F32)).astype(BF16)
    return a1, a2, a3


def _write_padded_heads(ref, rows_t, heads, width):
    tm = rows_t.shape[1]
    zeros = jnp.zeros((HEAD_DIM, tm), BF16)
    for hh in range(heads):
        off = HEAD_DIM * (hh % 2)
        ref[0, hh, off:off + HEAD_DIM, :] = rows_t[hh * width:hh * width + HEAD_DIM].astype(BF16)
        ref[0, hh, HEAD_DIM - off:2 * HEAD_DIM - off, :] = zeros


def _l0_in_kernel(x_ref, posr_ref, posc_ref, preg_ref, wn_ref, wt_ref, qag_ref, wqbt_ref,
                  kvag_ref, wk_ref, wvt_ref, invfr_ref, invfc_ref,
                  qtsb_ref, ksb_ref, vtsb_ref, gate_ref, qtmla_ref, kmla_ref, kxmla_ref, vtmla_ref):
    x = x_ref[0]
    h = _rms(x, preg_ref[...]).astype(BF16)

    def nat(a, b):
        return _dot(h, wn_ref[:, a:b])

    ksb_ref[0] = nat(0, 512).astype(BF16)
    gate_ref[0] = nat(512, 1536).astype(BF16)
    qa = nat(1536, 1920)
    ckv = nat(1920, 2176)
    kr = nat(2176, 2304)
    krr = nat(2304, 2432)

    tr = _dot_nt(wt_ref[...], h)
    _write_padded_heads(qtsb_ref, tr[0:SB_WIDTH], SB_HEADS, HEAD_DIM)
    vtsb_ref[0] = tr[SB_WIDTH:2 * SB_WIDTH].astype(BF16)

    qan = _rms(qa, qag_ref[...]).astype(BF16)
    qt = _dot_nt(wqbt_ref[...], qan)
    angt = invfc_ref[...] * posr_ref[0].astype(F32)
    cost = jnp.cos(angt)
    sint = jnp.sin(angt)
    tm = qt.shape[1]
    qtmla_ref[...] = jnp.zeros(qtmla_ref.shape, BF16)
    per_head = MLA_NOPE + MLA_ROPE
    half = MLA_ROPE // 2
    for hh in range(MLA_HEADS):
        base = hh * per_head
        off = MLA_NOPE * (hh % 2)
        qtmla_ref[0, hh, off:off + MLA_NOPE, :] = qt[base:base + MLA_NOPE].astype(BF16)
        x1 = qt[base + MLA_NOPE:base + MLA_NOPE + half]
        x2 = qt[base + MLA_NOPE + half:base + per_head]
        qtmla_ref[0, hh, LANES:LANES + half, :] = (x1 * cost - x2 * sint).astype(BF16)
        qtmla_ref[0, hh, LANES + half:LANES + MLA_ROPE, :] = (x2 * cost + x1 * sint).astype(BF16)

    ckvn = _rms(ckv, kvag_ref[...]).astype(BF16)
    kmla_ref[0] = _dot(ckvn, wk_ref[...]).astype(BF16)
    vtmla_ref[0] = _dot_nt(wvt_ref[...], ckvn).astype(BF16)
    ang = posc_ref[0].astype(F32) * invfr_ref[...]
    kxmla_ref[0] = (kr * jnp.cos(ang) + krr * jnp.sin(ang)).astype(BF16)


def _l0_in(x, positions, pre_g, wn, wt, qag, wqbt, kvag, wk, wvt, invf, tm):
    b, s, d = x.shape
    nt = s // tm
    posr = positions.reshape(b, 1, s)
    posc = positions.reshape(b, s, 1)
    invfr = jnp.tile(invf, LANES // invf.shape[0]).reshape(1, LANES)
    invfc = invf.reshape(-1, 1)
    const = lambda shape: pl.BlockSpec(shape, lambda bi, ti: (0,) * len(shape))
    row = lambda w: pl.BlockSpec((1, tm, w), lambda bi, ti: (bi, ti, 0))
    colt = lambda r: pl.BlockSpec((1, r, tm), lambda bi, ti: (bi, 0, ti))
    headt = lambda hn, r: pl.BlockSpec((1, hn, r, tm), lambda bi, ti: (bi, 0, 0, ti))
    out_shape = (
        jax.ShapeDtypeStruct((b, SB_HEADS, LANES, s), BF16),
        jax.ShapeDtypeStruct((b, s, SB_WIDTH), BF16),
        jax.ShapeDtypeStruct((b, SB_WIDTH, s), BF16),
        jax.ShapeDtypeStruct((b, s, SB_WIDTH + MLA_WIDTH), BF16),
        jax.ShapeDtypeStruct((b, MLA_HEADS, 2 * LANES, s), BF16),
        jax.ShapeDtypeStruct((b, s, MLA_WIDTH), BF16),
        jax.ShapeDtypeStruct((b, s, LANES), BF16),
        jax.ShapeDtypeStruct((b, MLA_WIDTH, s), BF16),
    )
    return pl.pallas_call(
        _l0_in_kernel,
        out_shape=out_shape,
        grid=(b, nt),
        in_specs=[
            row(d),
            pl.BlockSpec((1, 1, tm), lambda bi, ti: (bi, 0, ti)),
            pl.BlockSpec((1, tm, 1), lambda bi, ti: (bi, ti, 0)),
            const((1, d)), const(wn.shape), const(wt.shape), const((1, MLA_Q_LORA)), const(wqbt.shape),
            const((1, MLA_KV_LORA)), const(wk.shape), const(wvt.shape),
            const((1, LANES)), const((MLA_ROPE // 2, 1)),
        ],
        out_specs=(headt(SB_HEADS, LANES), row(SB_WIDTH), colt(SB_WIDTH), row(SB_WIDTH + MLA_WIDTH),
                   headt(MLA_HEADS, 2 * LANES), row(MLA_WIDTH), row(LANES), colt(MLA_WIDTH)),
        compiler_params=pltpu.CompilerParams(
            dimension_semantics=("arbitrary", "arbitrary"), vmem_limit_bytes=VMEM_LIMIT),
        name="l0_in",
    )(x, posr, posc, pre_g.reshape(1, d), wn, wt, qag.reshape(1, -1), wqbt,
      kvag.reshape(1, -1), wk, wvt, invfr, invfc)


def _l1_in_kernel(x_ref, preg_ref, wn_ref, wt_ref, bf_ref, tri_ref, place_ref,
                  qt_ref, k_ref, vt_ref, gate_ref, kx_ref, carry_ref):
    @pl.when(pl.program_id(1) == 0)
    def _():
        carry_ref[...] = jnp.zeros(carry_ref.shape, F32)

    x = x_ref[0]
    h = _rms(x, preg_ref[...]).astype(BF16)
    k_ref[0] = _dot(h, wn_ref[:, 0:1024]).astype(BF16)
    gate_ref[0] = _dot(h, wn_ref[:, 1024:2048]).astype(BF16)
    f = _dot(h, wn_ref[:, 2048:2176])[:, 0:FOX_HEADS] + bf_ref[...]

    tr = _dot_nt(wt_ref[...], h)
    _write_padded_heads(qt_ref, tr[0:FOX_WIDTH], FOX_HEADS, HEAD_DIM)
    vt_ref[0] = tr[FOX_WIDTH:2 * FOX_WIDTH].astype(BF16)

    logf = jnp.minimum(f, 0.0) - jnp.log1p(jnp.exp(-jnp.abs(f)))
    tri = tri_ref[...]
    c = carry_ref[0:1, 0:FOX_HEADS]
    for term in _split3(logf):
        c = c + _dot(tri, term)
    tm = c.shape[0]
    carry_ref[0:1, 0:FOX_HEADS] = c[tm - 1:tm, :]
    slab = jnp.zeros((tm, LANES), F32)
    for t, term in enumerate(_split3(-LOG2E * c)):
        slab = slab + _dot(term, place_ref[t])
    kx_ref[0] = slab.astype(BF16)


def _l1_in(x, pre_g, wn, wt, bf, tm):
    b, s, d = x.shape
    nt = s // tm
    tri = (lax.broadcasted_iota(jnp.int32, (tm, tm), 1)
           <= lax.broadcasted_iota(jnp.int32, (tm, tm), 0)).astype(BF16)
    hh = lax.broadcasted_iota(jnp.int32, (3, FOX_HEADS, LANES), 1)
    tt = lax.broadcasted_iota(jnp.int32, (3, FOX_HEADS, LANES), 0)
    ll = lax.broadcasted_iota(jnp.int32, (3, FOX_HEADS, LANES), 2)
    place = (ll == tt * FOX_HEADS + hh).astype(BF16)
    const = lambda shape: pl.BlockSpec(shape, lambda bi, ti: (0,) * len(shape))
    row = lambda w: pl.BlockSpec((1, tm, w), lambda bi, ti: (bi, ti, 0))
    out_shape = (
        jax.ShapeDtypeStruct((b, FOX_HEADS, LANES, s), BF16),
        jax.ShapeDtypeStruct((b, s, FOX_WIDTH), BF16),
        jax.ShapeDtypeStruct((b, FOX_WIDTH, s), BF16),
        jax.ShapeDtypeStruct((b, s, FOX_WIDTH), BF16),
        jax.ShapeDtypeStruct((b, s, LANES), BF16),
    )
    return pl.pallas_call(
        _l1_in_kernel,
        out_shape=out_shape,
        grid=(b, nt),
        in_specs=[row(d), const((1, d)), const(wn.shape), const(wt.shape),
                  const((1, FOX_HEADS)), const((tm, tm)), const(place.shape)],
        out_specs=(pl.BlockSpec((1, FOX_HEADS, LANES, tm), lambda bi, ti: (bi, 0, 0, ti)),
                   row(FOX_WIDTH),
                   pl.BlockSpec((1, FOX_WIDTH, tm), lambda bi, ti: (bi, 0, ti)),
                   row(FOX_WIDTH), row(LANES)),
        scratch_shapes=[pltpu.VMEM((8, LANES), F32)],
        compiler_params=pltpu.CompilerParams(
            dimension_semantics=("arbitrary", "arbitrary"), vmem_limit_bytes=VMEM_LIMIT),
        name="l1_in",
    )(x, pre_g.reshape(1, d), wn, wt, bf.reshape(1, -1), tri, place)


def _attn_kernel(*refs, mode, extra, tq, ch, dch):
    refs = list(refs)
    qt_ref = refs.pop(0)
    k_ref = refs.pop(0)
    kx_ref = refs.pop(0) if extra else None
    vt_ref = refs.pop(0)
    u_ref = refs.pop(0) if mode == "stick" else None
    o_ref, acc_ref, stat_ref, sbuf_ref = refs[:4]

    pi = pl.program_id(1)
    q0 = pl.program_id(2) * tq
    nd = tq // dch

    qts = []
    for hh in range(2):
        qt = qt_ref[0, hh]
        if extra == "bias":
            hg = 2 * pi + hh
            r = lax.broadcasted_iota(jnp.int32, (LANES, tq), 0)
            hit = (r == hg) | (r == hg + FOX_HEADS) | (r == hg + 2 * FOX_HEADS)
            qt = jnp.concatenate([qt, jnp.where(hit, 1.0, 0.0).astype(BF16)], axis=0)
        qts.append(qt)

    def k_chunk(start, n):
        k = k_ref[0, pl.ds(start, n), :]
        if extra:
            k = jnp.concatenate([k, kx_ref[0, pl.ds(start, n), :]], axis=1)
        return k

    def v_chunk(hh, start, n):
        return vt_ref[0, hh * HEAD_DIM:(hh + 1) * HEAD_DIM, pl.ds(start, n)]

    acc_ref[...] = jnp.zeros(acc_ref.shape, F32)
    rr = lax.broadcasted_iota(jnp.int32, (dch, dch), 0)
    cc = lax.broadcasted_iota(jnp.int32, (dch, dch), 1)

    def diag_ranges(c):
        off = c * dch
        out = [(off, off + dch, True)]
        if off + dch < tq:
            out.append((off + dch, tq, False))
        return out

    if mode == "softmax":
        nfull = q0 // ch
        per = ch // dch
        nb = tq // ch

        def with_ones(v):
            return jnp.concatenate([v, jnp.ones((SUM_ROWS, v.shape[1]), BF16)], axis=0)

        def produce(hh, slot, k):
            s = _dot(k, qts[hh])
            sbuf_ref[slot, hh] = s
            stat_ref[4 * hh + 2 + slot:4 * hh + 3 + slot, :] = jnp.max(s, axis=0, keepdims=True)

        for g in range(nb):
            kb = k_chunk(pl.multiple_of(q0 + g * ch, ch), ch)
            for hh in range(2):
                sbuf_ref[2 + g, hh, :, g * ch:] = _dot(kb, qts[hh][:, g * ch:])
        k0 = k_chunk(0, ch)
        for hh in range(2):
            produce(hh, 0, k0)

        def band_scores(hh, c, c0, c1, tri):
            g, r = divmod(c, per)
            s = sbuf_ref[2 + g, hh, r * dch:(r + 1) * dch, c0:c1]
            return jnp.where(rr <= cc, s, MASK_VALUE) if tri else s

        for hh in range(2):
            bmax = [None] * nd
            for c in range(nd):
                for c0, c1, tri in diag_ranges(c):
                    cm = jnp.max(band_scores(hh, c, c0, c1, tri), axis=0, keepdims=True)
                    for blk in range(c0 // dch, c1 // dch):
                        part = cm[:, blk * dch - c0:(blk + 1) * dch - c0]
                        bmax[blk] = part if bmax[blk] is None else jnp.maximum(bmax[blk], part)
            m_band = jnp.concatenate(bmax, axis=1)
            stat_ref[4 * hh:4 * hh + 1, :] = m_band
            for c in range(nd):
                off = c * dch
                ps = [jnp.exp2(band_scores(hh, c, c0, c1, tri) - m_band[:, c0:c1])
                      for c0, c1, tri in diag_ranges(c)]
                p = ps[0] if len(ps) == 1 else jnp.concatenate(ps, axis=1)
                pv = _dot(with_ones(v_chunk(hh, pl.multiple_of(q0 + off, dch), dch)), p.astype(BF16))
                if c == 0:
                    acc_ref[hh] = pv
                else:
                    acc_ref[hh, :, off:] = acc_ref[hh, :, off:] + pv

        def half_step(j, slot, prefetch):
            start = pl.multiple_of(j * ch, ch)
            if prefetch:
                kn = k_chunk(pl.multiple_of(start + ch, ch), ch)
            for hh in range(2):
                if prefetch:
                    produce(hh, 1 - slot, kn)
                m_prev = stat_ref[4 * hh:4 * hh + 1, :]
                m_new = jnp.maximum(m_prev, stat_ref[4 * hh + 2 + slot:4 * hh + 3 + slot, :])
                stat_ref[4 * hh:4 * hh + 1, :] = m_new
                p = jnp.exp2(sbuf_ref[slot, hh] - m_new)
                pv = _dot(with_ones(v_chunk(hh, start, ch)), p.astype(BF16))
                acc_ref[hh] = jnp.exp2(m_prev - m_new) * acc_ref[hh] + pv

        def body(i, carry):
            half_step(2 * i, 0, True)
            half_step(2 * i + 1, 1, True)
            return carry

        lax.fori_loop(0, nfull // 2 - 1, body, 0)

        @pl.when(nfull > 0)
        def _():
            half_step(nfull - 2, 0, True)
            half_step(nfull - 1, 1, False)

        o_t = jnp.concatenate([acc_ref[hh, 0:HEAD_DIM, :] / acc_ref[hh, HEAD_DIM:HEAD_DIM + 1, :]
                               for hh in range(2)], axis=0)
    else:
        hl_ref = refs[4]
        nfull = q0 // dch
        stat_ref[...] = jnp.zeros(stat_ref.shape, F32)

        def mask_first(x, fill):
            first = jnp.where(rr < cc, x[:, 0:dch], fill)
            return first if x.shape[1] == dch else jnp.concatenate([first, x[:, dch:]], axis=1)

        def produce(hh, slot, k, off, diag):
            z = _dot(k, qts[hh][:, off:])
            sp = jnp.maximum(z, 0.0) + jnp.log(1.0 + jnp.exp2(jnp.abs(z) * -LOG2E))
            if diag:
                sp = mask_first(sp, 0.0)
            sbuf_ref[slot, hh, :, off:] = z - sp
            hl_ref[slot, hh, :, off:] = sp.astype(BF16)
            stat_ref[2 + 2 * slot + hh:3 + 2 * slot + hh, off:] = sp[0:1, :]

        def suffix_sums(hh, slot, off):
            u = u_ref[...]
            return _dot(u, hl_ref[slot, hh, :, off:])

        def consume(hh, slot, remain, start, off, diag):
            carry = stat_ref[hh:hh + 1, off:]
            first = stat_ref[2 + 2 * slot + hh:3 + 2 * slot + hh, off:]
            stat_ref[hh:hh + 1, off:] = carry + remain[0:1, :] - first
            w = jnp.exp(sbuf_ref[slot, hh, :, off:] + remain + carry)
            if diag:
                w = mask_first(w, 0.0)
            acc_ref[hh, :, off:] = acc_ref[hh, :, off:] + _dot(v_chunk(hh, start, dch), w.astype(BF16))

        for c in reversed(range(nd)):
            kc = k_chunk(pl.multiple_of(q0 + c * dch, dch), dch)
            for hh in range(2):
                produce(hh, 2 + c, kc, c * dch, True)
        kl = k_chunk(pl.multiple_of(jnp.maximum(nfull - 1, 0) * dch, dch), dch)
        for hh in range(2):
            produce(hh, 1, kl, 0, False)
        ahead = [suffix_sums(hh, 1 + nd, (nd - 1) * dch) for hh in range(2)]
        for c in reversed(range(nd)):
            remain = ahead
            if c > 0:
                ahead = [suffix_sums(hh, 1 + c, (c - 1) * dch) for hh in range(2)]
            for hh in range(2):
                consume(hh, 2 + c, remain[hh], pl.multiple_of(q0 + c * dch, dch), c * dch, True)

        def half_step(j, slot, prefetch):
            start = pl.multiple_of(j * dch, dch)
            if prefetch:
                kn = k_chunk(pl.multiple_of((j - 1) * dch, dch), dch)
            for hh in range(2):
                remain = suffix_sums(hh, slot, 0)
                if prefetch:
                    produce(hh, 1 - slot, kn, 0, False)
                consume(hh, slot, remain, start, 0, False)

        def body(t, carry):
            j = nfull - 1 - 2 * t
            half_step(j, 1, True)
            half_step(j - 1, 0, True)
            return carry

        lax.fori_loop(0, nfull // 2 - 1, body, 0)

        @pl.when(nfull > 0)
        def _():
            half_step(1, 1, True)
            half_step(0, 0, False)

        o_t = jnp.concatenate([acc_ref[0], acc_ref[1]], axis=0)

    o_ref[0] = o_t.T.astype(o_ref.dtype)


def _attention(qt, k, vt, kx, *, mode, extra, tq, ch, dch):
    b, s, width = k.shape
    heads = qt.shape[1]
    kd = qt.shape[2]
    in_specs = [pl.BlockSpec((1, 2, kd, tq), lambda bi, pi, qi: (bi, pi, 0, qi)),
                pl.BlockSpec((1, s, LANES), lambda bi, pi, qi: (bi, 0, pi))]
    args = [qt, k]
    if extra:
        in_specs.append(pl.BlockSpec((1, s, LANES), lambda bi, pi, qi: (bi, 0, 0)))
        args.append(kx)
    in_specs.append(pl.BlockSpec((1, LANES, s), lambda bi, pi, qi: (bi, pi, 0)))
    args.append(vt)
    acc_rows = HEAD_DIM + (SUM_ROWS if mode == "softmax" else 0)
    nslot = 2 + tq // dch
    stat_rows = 8 * pl.cdiv(2 + 2 * nslot, 8)
    scratch = [pltpu.VMEM((2, acc_rows, tq), F32), pltpu.VMEM((stat_rows, tq), F32)]
    if mode == "softmax":
        scratch.append(pltpu.VMEM((2 + tq // ch, 2, ch, tq), F32))
    else:
        upper = jnp.where(lax.broadcasted_iota(jnp.int32, (dch, dch), 1)
                          > lax.broadcasted_iota(jnp.int32, (dch, dch), 0), -1.0, 0.0).astype(BF16)
        in_specs.append(pl.BlockSpec((dch, dch), lambda bi, pi, qi: (0, 0)))
        args.append(upper)
        scratch += [pltpu.VMEM((nslot, 2, dch, tq), F32),
                    pltpu.VMEM((nslot, 2, dch, tq), BF16)]
    kernel = functools.partial(_attn_kernel, mode=mode, extra=extra, tq=tq, ch=ch, dch=dch)
    return pl.pallas_call(
        kernel,
        out_shape=jax.ShapeDtypeStruct((b, s, width), BF16),
        grid=(b, heads // 2, s // tq),
        in_specs=in_specs,
        out_specs=pl.BlockSpec((1, tq, LANES), lambda bi, pi, qi: (bi, qi, pi)),
        scratch_shapes=scratch,
        compiler_params=pltpu.CompilerParams(
            dimension_semantics=("arbitrary", "arbitrary", "arbitrary"), vmem_limit_bytes=VMEM_LIMIT),
        name="attn_" + mode + ("_" + extra if extra else ""),
    )(*args)


def _out_kernel(*refs, n_o):
    o_refs = refs[:n_o]
    gate_ref, x_ref, w_ref, g_ref, out_ref = refs[n_o:]
    gate = _silu(gate_ref[...].astype(F32))
    y = None
    col = 0
    for o_ref in o_refs:
        wdt = o_ref.shape[1]
        og = (o_ref[...].astype(F32) * gate[:, col:col + wdt]).astype(BF16)
        part = _dot(og, w_ref[col:col + wdt, :])
        y = part if y is None else y + part
        col += wdt
    out_ref[...] = x_ref[...] + _rms(y, g_ref[...])


def _out_stage(os, gate, x, w, g, tm):
    n, d = x.shape
    os = [o.reshape(n, o.shape[-1]) for o in os]
    in_specs = [pl.BlockSpec((tm, o.shape[1]), lambda i: (i, 0)) for o in os]
    in_specs += [pl.BlockSpec((tm, gate.shape[1]), lambda i: (i, 0)),
                 pl.BlockSpec((tm, d), lambda i: (i, 0)),
                 pl.BlockSpec(w.shape, lambda i: (0, 0)),
                 pl.BlockSpec((1, d), lambda i: (0, 0))]
    return pl.pallas_call(
        functools.partial(_out_kernel, n_o=len(os)),
        out_shape=jax.ShapeDtypeStruct((n, d), F32),
        grid=(n // tm,),
        in_specs=in_specs,
        out_specs=pl.BlockSpec((tm, d), lambda i: (i, 0)),
        compiler_params=pltpu.CompilerParams(
            dimension_semantics=("arbitrary",), vmem_limit_bytes=VMEM_LIMIT),
        name="out_stage",
    )(*os, gate, x, w, g.reshape(1, d))


def kernel(x, positions, l0_pre_g, l0_post_g, l0_w_in, l0_q_a_g, l0_w_q_b, l0_kv_a_g, l0_w_kv_b, l0_w_out,
           l1_pre_g, l1_post_g, l1_w_in, l1_b_f, l1_w_out):
    b, s, d = x.shape
    tm = min(ROW_TILE, s)
    tq = min(Q_TILE, s)
    dch = min(DIAG_CHUNK, tq // 2)
    ch = min(SOFTMAX_CHUNK, tq // 2)
    assert tq % (2 * ch) == 0 and ch % dch == 0 and s % tq == 0
    sb_scale = HEAD_DIM ** -0.5
    mla_scale = (MLA_NOPE + MLA_ROPE) ** -0.5

    w = l0_w_in
    c0 = 4 * SB_WIDTH
    c1 = c0 + MLA_Q_LORA
    c2 = c1 + MLA_KV_LORA
    c3 = c2 + MLA_ROPE
    half = MLA_ROPE // 2
    wkr = w[:, c2:c3]
    wkr_rot = jnp.concatenate([-wkr[:, half:], wkr[:, :half]], axis=1)
    lane_pad = jnp.zeros((d, LANES - MLA_ROPE), F32)
    wn0 = jnp.concatenate([w[:, 512:1024], w[:, 1536:2048], w[:, c3:], w[:, c0:c1], w[:, c1:c2],
                           wkr, lane_pad, wkr_rot, lane_pad], axis=1).astype(BF16)
    wt0 = jnp.concatenate([w[:, 0:512] * sb_scale, w[:, 1024:1536]], axis=1).T.astype(BF16)
    wqbt = (l0_w_q_b * (mla_scale * LOG2E)).T.astype(BF16)
    wkv = l0_w_kv_b.reshape(MLA_KV_LORA, MLA_HEADS, MLA_NOPE + MLA_V)
    wk = wkv[:, :, :MLA_NOPE].reshape(MLA_KV_LORA, -1).astype(BF16)
    wvt = wkv[:, :, MLA_NOPE:].reshape(MLA_KV_LORA, -1).T.astype(BF16)
    invf = ROPE_THETA ** (-jnp.arange(0, MLA_ROPE, 2, dtype=F32) / MLA_ROPE)

    qt_sb, k_sb, vt_sb, gate0, qt_mla, k_mla, kx_mla, vt_mla = _l0_in(
        x, positions, l0_pre_g, wn0, wt0, l0_q_a_g, wqbt, l0_kv_a_g, wk, wvt, invf, tm)
    o_sb = _attention(qt_sb, k_sb, vt_sb, None, mode="stick", extra=None, tq=tq, ch=ch, dch=dch)
    o_mla = _attention(qt_mla, k_mla, vt_mla, kx_mla, mode="softmax", extra="rows", tq=tq, ch=ch, dch=dch)
    x1 = _out_stage([o_sb, o_mla], gate0.reshape(b * s, -1), x.reshape(b * s, d),
                    l0_w_out.astype(BF16), l0_post_g, tm)

    w1 = l1_w_in
    f_pad = jnp.zeros((d, LANES - FOX_HEADS), F32)
    wn1 = jnp.concatenate([w1[:, 1024:2048], w1[:, 3072:4096], w1[:, 4096:], f_pad], axis=1).astype(BF16)
    wt1 = jnp.concatenate([w1[:, 0:1024] * (sb_scale * LOG2E), w1[:, 2048:3072]], axis=1).T.astype(BF16)
    qt1, k1, vt1, gate1, kx1 = _l1_in(x1.reshape(b, s, d), l1_pre_g, wn1, wt1, l1_b_f, tm)
    o1 = _attention(qt1, k1, vt1, kx1, mode="softmax", extra="bias", tq=tq, ch=ch, dch=dch)
    out = _out_stage([o1], gate1.reshape(b * s, -1), x1, l1_w_out.astype(BF16), l1_post_g, tm)
    return out.reshape(b, s, d)
```

```python
import functools
import math

import jax
import jax.numpy as jnp
from jax import lax
from jax.experimental import pallas as pl
from jax.experimental.pallas import tpu as pltpu

D_MODEL = 1024
RMS_EPS = 1e-6
HEAD_DIM = 64
SB_HEADS = 8
SB_WIDTH = SB_HEADS * HEAD_DIM
MLA_HEADS = 8
MLA_NOPE = 64
MLA_ROPE = 32
MLA_V = 64
MLA_Q_LORA = 384
MLA_KV_LORA = 256
MLA_WIDTH = MLA_HEADS * MLA_V
ROPE_THETA = 10000.0
FOX_HEADS = 16
FOX_WIDTH = FOX_HEADS * HEAD_DIM

LANES = 128
ROW_TILE = 512
Q_TILE = 1024
SOFTMAX_CHUNK = 512
DIAG_CHUNK = 256
SUM_ROWS = 16
MASK_VALUE = -1e30
STICK_DEAD = -104.0
LOG2E = math.log2(math.e)
VMEM_LIMIT = 48 * 1024 * 1024
F32 = jnp.float32
BF16 = jnp.bfloat16


def _rms(xf, g):
    var = jnp.mean(xf * xf, axis=-1, keepdims=True)
    return xf * lax.rsqrt(var + RMS_EPS) * g


def _dot(a, b):
    return jnp.dot(a, b, preferred_element_type=F32)


def _dot_nt(a, b):
    return lax.dot_general(a, b, (((1,), (1,)), ((), ())), preferred_element_type=F32)


def _silu(g):
    return g / (1.0 + jnp.exp(-g))


def _split3(v):
    a1 = v.astype(BF16)
    r1 = v - a1.astype(F32)
    a2 = r1.astype(BF16)
    a3 = (r1 - a2.astype(F32)).astype(BF16)
    return a1, a2, a3


def _write_padded_heads(ref, rows_t, heads, width):
    tm = rows_t.shape[1]
    zeros = jnp.zeros((HEAD_DIM, tm), BF16)
    for hh in range(heads):
        off = HEAD_DIM * (hh % 2)
        ref[0, hh, off:off + HEAD_DIM, :] = rows_t[hh * width:hh * width + HEAD_DIM].astype(BF16)
        ref[0, hh, HEAD_DIM - off:2 * HEAD_DIM - off, :] = zeros


def _l0_in_kernel(x_ref, posr_ref, posc_ref, preg_ref, wn_ref, wt_ref, qag_ref, wqbt_ref,
                  kvag_ref, wk_ref, wvt_ref, invfr_ref, invfc_ref,
                  qtsb_ref, ksb_ref, vtsb_ref, gate_ref, qtmla_ref, kmla_ref, kxmla_ref, vtmla_ref):
    x = x_ref[0]
    h = _rms(x, preg_ref[...]).astype(BF16)

    def nat(a, b):
        return _dot(h, wn_ref[:, a:b])

    ksb_ref[0] = nat(0, 512).astype(BF16)
    gate_ref[0] = nat(512, 1536).astype(BF16)
    qa = nat(1536, 1920)
    ckv = nat(1920, 2176)
    kr = nat(2176, 2304)
    krr = nat(2304, 2432)

    tr = _dot_nt(wt_ref[...], h)
    _write_padded_heads(qtsb_ref, tr[0:SB_WIDTH], SB_HEADS, HEAD_DIM)
    vtsb_ref[0] = tr[SB_WIDTH:2 * SB_WIDTH].astype(BF16)

    qan = _rms(qa, qag_ref[...]).astype(BF16)
    qt = _dot_nt(wqbt_ref[...], qan)
    angt = invfc_ref[...] * posr_ref[0].astype(F32)
    cost = jnp.cos(angt)
    sint = jnp.sin(angt)
    tm = qt.shape[1]
    qtmla_ref[...] = jnp.zeros(qtmla_ref.shape, BF16)
    per_head = MLA_NOPE + MLA_ROPE
    half = MLA_ROPE // 2
    for hh in range(MLA_HEADS):
        base = hh * per_head
        off = MLA_NOPE * (hh % 2)
        qtmla_ref[0, hh, off:off + MLA_NOPE, :] = qt[base:base + MLA_NOPE].astype(BF16)
        x1 = qt[base + MLA_NOPE:base + MLA_NOPE + half]
        x2 = qt[base + MLA_NOPE + half:base + per_head]
        qtmla_ref[0, hh, LANES:LANES + half, :] = (x1 * cost - x2 * sint).astype(BF16)
        qtmla_ref[0, hh, LANES + half:LANES + MLA_ROPE, :] = (x2 * cost + x1 * sint).astype(BF16)

    ckvn = _rms(ckv, kvag_ref[...]).astype(BF16)
    kmla_ref[0] = _dot(ckvn, wk_ref[...]).astype(BF16)
    vtmla_ref[0] = _dot_nt(wvt_ref[...], ckvn).astype(BF16)
    ang = posc_ref[0].astype(F32) * invfr_ref[...]
    kxmla_ref[0] = (kr * jnp.cos(ang) + krr * jnp.sin(ang)).astype(BF16)


def _l0_in(x, positions, pre_g, wn, wt, qag, wqbt, kvag, wk, wvt, invf, tm):
    b, s, d = x.shape
    nt = s // tm
    posr = positions.reshape(b, 1, s)
    posc = positions.reshape(b, s, 1)
    invfr = jnp.tile(invf, LANES // invf.shape[0]).reshape(1, LANES)
    invfc = invf.reshape(-1, 1)
    const = lambda shape: pl.BlockSpec(shape, lambda bi, ti: (0,) * len(shape))
    row = lambda w: pl.BlockSpec((1, tm, w), lambda bi, ti: (bi, ti, 0))
    colt = lambda r: pl.BlockSpec((1, r, tm), lambda bi, ti: (bi, 0, ti))
    headt = lambda hn, r: pl.BlockSpec((1, hn, r, tm), lambda bi, ti: (bi, 0, 0, ti))
    out_shape = (
        jax.ShapeDtypeStruct((b, SB_HEADS, LANES, s), BF16),
        jax.ShapeDtypeStruct((b, s, SB_WIDTH), BF16),
        jax.ShapeDtypeStruct((b, SB_WIDTH, s), BF16),
        jax.ShapeDtypeStruct((b, s, SB_WIDTH + MLA_WIDTH), BF16),
        jax.ShapeDtypeStruct((b, MLA_HEADS, 2 * LANES, s), BF16),
        jax.ShapeDtypeStruct((b, s, MLA_WIDTH), BF16),
        jax.ShapeDtypeStruct((b, s, LANES), BF16),
        jax.ShapeDtypeStruct((b, MLA_WIDTH, s), BF16),
    )
    return pl.pallas_call(
        _l0_in_kernel,
        out_shape=out_shape,
        grid=(b, nt),
        in_specs=[
            row(d),
            pl.BlockSpec((1, 1, tm), lambda bi, ti: (bi, 0, ti)),
            pl.BlockSpec((1, tm, 1), lambda bi, ti: (bi, ti, 0)),
            const((1, d)), const(wn.shape), const(wt.shape), const((1, MLA_Q_LORA)), const(wqbt.shape),
            const((1, MLA_KV_LORA)), const(wk.shape), const(wvt.shape),
            const((1, LANES)), const((MLA_ROPE // 2, 1)),
        ],
        out_specs=(headt(SB_HEADS, LANES), row(SB_WIDTH), colt(SB_WIDTH), row(SB_WIDTH + MLA_WIDTH),
                   headt(MLA_HEADS, 2 * LANES), row(MLA_WIDTH), row(LANES), colt(MLA_WIDTH)),
        compiler_params=pltpu.CompilerParams(
            dimension_semantics=("arbitrary", "arbitrary"), vmem_limit_bytes=VMEM_LIMIT),
        name="l0_in",
    )(x, posr, posc, pre_g.reshape(1, d), wn, wt, qag.reshape(1, -1), wqbt,
      kvag.reshape(1, -1), wk, wvt, invfr, invfc)


def _l1_in_kernel(x_ref, preg_ref, wn_ref, wt_ref, bf_ref, tri_ref, place_ref,
                  qt_ref, k_ref, vt_ref, gate_ref, kx_ref, carry_ref):
    @pl.when(pl.program_id(1) == 0)
    def _():
        carry_ref[...] = jnp.zeros(carry_ref.shape, F32)

    x = x_ref[0]
    h = _rms(x, preg_ref[...]).astype(BF16)
    k_ref[0] = _dot(h, wn_ref[:, 0:1024]).astype(BF16)
    gate_ref[0] = _dot(h, wn_ref[:, 1024:2048]).astype(BF16)
    f = _dot(h, wn_ref[:, 2048:2176])[:, 0:FOX_HEADS] + bf_ref[...]

    tr = _dot_nt(wt_ref[...], h)
    _write_padded_heads(qt_ref, tr[0:FOX_WIDTH], FOX_HEADS, HEAD_DIM)
    vt_ref[0] = tr[FOX_WIDTH:2 * FOX_WIDTH].astype(BF16)

    logf = jnp.minimum(f, 0.0) - jnp.log1p(jnp.exp(-jnp.abs(f)))
    tri = tri_ref[...]
    c = carry_ref[0:1, 0:FOX_HEADS]
    for term in _split3(logf):
        c = c + _dot(tri, term)
    tm = c.shape[0]
    carry_ref[0:1, 0:FOX_HEADS] = c[tm - 1:tm, :]
    slab = jnp.zeros((tm, LANES), F32)
    for t, term in enumerate(_split3(-LOG2E * c)):
        slab = slab + _dot(term, place_ref[t])
    kx_ref[0] = slab.astype(BF16)


def _l1_in(x, pre_g, wn, wt, bf, tm):
    b, s, d = x.shape
    nt = s // tm
    tri = (lax.broadcasted_iota(jnp.int32, (tm, tm), 1)
           <= lax.broadcasted_iota(jnp.int32, (tm, tm), 0)).astype(BF16)
    hh = lax.broadcasted_iota(jnp.int32, (3, FOX_HEADS, LANES), 1)
    tt = lax.broadcasted_iota(jnp.int32, (3, FOX_HEADS, LANES), 0)
    ll = lax.broadcasted_iota(jnp.int32, (3, FOX_HEADS, LANES), 2)
    place = (ll == tt * FOX_HEADS + hh).astype(BF16)
    const = lambda shape: pl.BlockSpec(shape, lambda bi, ti: (0,) * len(shape))
    row = lambda w: pl.BlockSpec((1, tm, w), lambda bi, ti: (bi, ti, 0))
    out_shape = (
        jax.ShapeDtypeStruct((b, FOX_HEADS, LANES, s), BF16),
        jax.ShapeDtypeStruct((b, s, FOX_WIDTH), BF16),
        jax.ShapeDtypeStruct((b, FOX_WIDTH, s), BF16),
        jax.ShapeDtypeStruct((b, s, FOX_WIDTH), BF16),
        jax.ShapeDtypeStruct((b, s, LANES), BF16),
    )
    return pl.pallas_call(
        _l1_in_kernel,
        out_shape=out_shape,
        grid=(b, nt),
        in_specs=[row(d), const((1, d)), const(wn.shape), const(wt.shape),
                  const((1, FOX_HEADS)), const((tm, tm)), const(place.shape)],
        out_specs=(pl.BlockSpec((1, FOX_HEADS, LANES, tm), lambda bi, ti: (bi, 0, 0, ti)),
                   row(FOX_WIDTH),
                   pl.BlockSpec((1, FOX_WIDTH, tm), lambda bi, ti: (bi, 0, ti)),
                   row(FOX_WIDTH), row(LANES)),
        scratch_shapes=[pltpu.VMEM((8, LANES), F32)],
        compiler_params=pltpu.CompilerParams(
            dimension_semantics=("arbitrary", "arbitrary"), vmem_limit_bytes=VMEM_LIMIT),
        name="l1_in",
    )(x, pre_g.reshape(1, d), wn, wt, bf.reshape(1, -1), tri, place)


def _attn_kernel(*refs, mode, extra, tq, ch, dch):
    refs = list(refs)
    qt_ref = refs.pop(0)
    k_ref = refs.pop(0)
    kx_ref = refs.pop(0) if extra else None
    vt_ref = refs.pop(0)
    u_ref = refs.pop(0) if mode == "stick" else None
    o_ref, acc_ref, stat_ref, sbuf_ref = refs[:4]

    pi = pl.program_id(1)
    q0 = pl.program_id(2) * tq
    nd = tq // dch

    qts = []
    for hh in range(2):
        qt = qt_ref[0, hh]
        if extra == "bias":
            hg = 2 * pi + hh
            r = lax.broadcasted_iota(jnp.int32, (LANES, tq), 0)
            hit = (r == hg) | (r == hg + FOX_HEADS) | (r == hg + 2 * FOX_HEADS)
            qt = jnp.concatenate([qt, jnp.where(hit, 1.0, 0.0).astype(BF16)], axis=0)
        qts.append(qt)

    def k_chunk(start, n):
        k = k_ref[0, pl.ds(start, n), :]
        if extra:
            k = jnp.concatenate([k, kx_ref[0, pl.ds(start, n), :]], axis=1)
        return k

    def v_chunk(hh, start, n):
        return vt_ref[0, hh * HEAD_DIM:(hh + 1) * HEAD_DIM, pl.ds(start, n)]

    acc_ref[...] = jnp.zeros(acc_ref.shape, F32)
    rr = lax.broadcasted_iota(jnp.int32, (dch, dch), 0)
    cc = lax.broadcasted_iota(jnp.int32, (dch, dch), 1)

    def diag_ranges(c):
        off = c * dch
        out = [(off, off + dch, True)]
        if off + dch < tq:
            out.append((off + dch, tq, False))
        return out

    if mode == "softmax":
        nfull = q0 // ch
        per = ch // dch
        nb = tq // ch

        def with_ones(v):
            return jnp.concatenate([v, jnp.ones((SUM_ROWS, v.shape[1]), BF16)], axis=0)

        def produce(hh, slot, k):
            s = _dot(k, qts[hh])
            sbuf_ref[slot, hh] = s
            stat_ref[4 * hh + 2 + slot:4 * hh + 3 + slot, :] = jnp.max(s, axis=0, keepdims=True)

        for g in range(nb):
            kb = k_chunk(pl.multiple_of(q0 + g * ch, ch), ch)
            for hh in range(2):
                sbuf_ref[2 + g, hh, :, g * ch:] = _dot(kb, qts[hh][:, g * ch:])
        k0 = k_chunk(0, ch)
        for hh in range(2):
            produce(hh, 0, k0)

        def band_scores(hh, c, c0, c1, tri):
            g, r = divmod(c, per)
            s = sbuf_ref[2 + g, hh, r * dch:(r + 1) * dch, c0:c1]
            return jnp.where(rr <= cc, s, MASK_VALUE) if tri else s

        for hh in range(2):
            bmax = [None] * nd
            for c in range(nd):
                for c0, c1, tri in diag_ranges(c):
                    cm = jnp.max(band_scores(hh, c, c0, c1, tri), axis=0, keepdims=True)
                    for blk in range(c0 // dch, c1 // dch):
                        part = cm[:, blk * dch - c0:(blk + 1) * dch - c0]
                        bmax[blk] = part if bmax[blk] is None else jnp.maximum(bmax[blk], part)
            m_band = jnp.concatenate(bmax, axis=1)
            stat_ref[4 * hh:4 * hh + 1, :] = m_band
            for c in range(nd):
                off = c * dch
                ps = [jnp.exp2(band_scores(hh, c, c0, c1, tri) - m_band[:, c0:c1])
                      for c0, c1, tri in diag_ranges(c)]
                p = ps[0] if len(ps) == 1 else jnp.concatenate(ps, axis=1)
                pv = _dot(with_ones(v_chunk(hh, pl.multiple_of(q0 + off, dch), dch)), p.astype(BF16))
                if c == 0:
                    acc_ref[hh] = pv
                else:
                    acc_ref[hh, :, off:] = acc_ref[hh, :, off:] + pv

        def half_step(j, slot, prefetch):
            start = pl.multiple_of(j * ch, ch)
            if prefetch:
                kn = k_chunk(pl.multiple_of(start + ch, ch), ch)
            for hh in range(2):
                if prefetch:
                    produce(hh, 1 - slot, kn)
                m_prev = stat_ref[4 * hh:4 * hh + 1, :]
                m_new = jnp.maximum(m_prev, stat_ref[4 * hh + 2 + slot:4 * hh + 3 + slot, :])
                stat_ref[4 * hh:4 * hh + 1, :] = m_new
                p = jnp.exp2(sbuf_ref[slot, hh] - m_new)
                pv = _dot(with_ones(v_chunk(hh, start, ch)), p.astype(BF16))
                acc_ref[hh] = jnp.exp2(m_prev - m_new) * acc_ref[hh] + pv

        def body(i, carry):
            half_step(2 * i, 0, True)
            half_step(2 * i + 1, 1, True)
            return carry

        lax.fori_loop(0, nfull // 2 - 1, body, 0)

        @pl.when(nfull > 0)
        def _():
            half_step(nfull - 2, 0, True)
            half_step(nfull - 1, 1, False)

        o_t = jnp.concatenate([acc_ref[hh, 0:HEAD_DIM, :] / acc_ref[hh, HEAD_DIM:HEAD_DIM + 1, :]
                               for hh in range(2)], axis=0)
    else:
        hl_ref = refs[4]
        nfull = q0 // dch
        stat_ref[...] = jnp.zeros(stat_ref.shape, F32)

        def mask_first(x, fill):
            first = jnp.where(rr < cc, x[:, 0:dch], fill)
            return first if x.shape[1] == dch else jnp.concatenate([first, x[:, dch:]], axis=1)

        def produce(hh, slot, k, off, diag):
            z = _dot(k, qts[hh][:, off:])
            sp = jnp.maximum(z, 0.0) + jnp.log(1.0 + jnp.exp2(jnp.abs(z) * -LOG2E))
            if diag:
                sp = mask_first(sp, 0.0)
            sbuf_ref[slot, hh, :, off:] = z - sp
            hl_ref[slot, hh, :, off:] = sp.astype(BF16)
            stat_ref[2 + 2 * slot + hh:3 + 2 * slot + hh, off:] = sp[0:1, :]

        def suffix_sums(hh, slot, off):
            u = u_ref[...]
            return _dot(u, hl_ref[slot, hh, :, off:])

        def consume(hh, slot, remain, start, off, diag):
            carry = stat_ref[hh:hh + 1, off:]
            first = stat_ref[2 + 2 * slot + hh:3 + 2 * slot + hh, off:]
            stat_ref[hh:hh + 1, off:] = carry + remain[0:1, :] - first
            w = jnp.exp(sbuf_ref[slot, hh, :, off:] + remain + carry)
            if diag:
                w = mask_first(w, 0.0)
            acc_ref[hh, :, off:] = acc_ref[hh, :, off:] + _dot(v_chunk(hh, start, dch), w.astype(BF16))

        for c in reversed(range(nd)):
            kc = k_chunk(pl.multiple_of(q0 + c * dch, dch), dch)
            for hh in range(2):
                produce(hh, 2 + c, kc, c * dch, True)
        kl = k_chunk(pl.multiple_of(jnp.maximum(nfull - 1, 0) * dch, dch), dch)
        for hh in range(2):
            produce(hh, 1, kl, 0, False)
        ahead = [suffix_sums(hh, 1 + nd, (nd - 1) * dch) for hh in range(2)]
        for c in reversed(range(nd)):
            remain = ahead
            if c > 0:
                ahead = [suffix_sums(hh, 1 + c, (c - 1) * dch) for hh in range(2)]
            for hh in range(2):
                consume(hh, 2 + c, remain[hh], pl.multiple_of(q0 + c * dch, dch), c * dch, True)

        def half_step(j, slot, prefetch):
            start = pl.multiple_of(j * dch, dch)
            if prefetch:
                kn = k_chunk(pl.multiple_of((j - 1) * dch, dch), dch)
            for hh in range(2):
                remain = suffix_sums(hh, slot, 0)
                if prefetch:
                    produce(hh, 1 - slot, kn, 0, False)
                consume(hh, slot, remain, start, 0, False)

        def stick_left():
            return (jnp.max(stat_ref[0:2, :]) > STICK_DEAD).astype(jnp.int32)

        def pair(state):
            t, _ = state
            j = nfull - 1 - 2 * t
            half_step(j, 1, True)
            half_step(j - 1, 0, True)
            return t + 1, stick_left()

        npair = nfull // 2
        _, alive = lax.while_loop(lambda st: (st[0] < npair - 1) & (st[1] > 0), pair,
                                  (jnp.int32(0), stick_left()))

        @pl.when((nfull > 0) & (alive > 0))
        def _():
            half_step(1, 1, True)
            half_step(0, 0, False)

        o_t = jnp.concatenate([acc_ref[0], acc_ref[1]], axis=0)

    o_ref[0] = o_t.T.astype(o_ref.dtype)


def _attention(qt, k, vt, kx, *, mode, extra, tq, ch, dch):
    b, s, width = k.shape
    heads = qt.shape[1]
    kd = qt.shape[2]
    in_specs = [pl.BlockSpec((1, 2, kd, tq), lambda bi, pi, qi: (bi, pi, 0, qi)),
                pl.BlockSpec((1, s, LANES), lambda bi, pi, qi: (bi, 0, pi))]
    args = [qt, k]
    if extra:
        in_specs.append(pl.BlockSpec((1, s, LANES), lambda bi, pi, qi: (bi, 0, 0)))
        args.append(kx)
    in_specs.append(pl.BlockSpec((1, LANES, s), lambda bi, pi, qi: (bi, pi, 0)))
    args.append(vt)
    acc_rows = HEAD_DIM + (SUM_ROWS if mode == "softmax" else 0)
    nslot = 2 + tq // dch
    stat_rows = 8 * pl.cdiv(2 + 2 * nslot, 8)
    scratch = [pltpu.VMEM((2, acc_rows, tq), F32), pltpu.VMEM((stat_rows, tq), F32)]
    if mode == "softmax":
        scratch.append(pltpu.VMEM((2 + tq // ch, 2, ch, tq), F32))
    else:
        upper = jnp.where(lax.broadcasted_iota(jnp.int32, (dch, dch), 1)
                          > lax.broadcasted_iota(jnp.int32, (dch, dch), 0), -1.0, 0.0).astype(BF16)
        in_specs.append(pl.BlockSpec((dch, dch), lambda bi, pi, qi: (0, 0)))
        args.append(upper)
        scratch += [pltpu.VMEM((nslot, 2, dch, tq), F32),
                    pltpu.VMEM((nslot, 2, dch, tq), BF16)]
    kernel = functools.partial(_attn_kernel, mode=mode, extra=extra, tq=tq, ch=ch, dch=dch)
    return pl.pallas_call(
        kernel,
        out_shape=jax.ShapeDtypeStruct((b, s, width), BF16),
        grid=(b, heads // 2, s // tq),
        in_specs=in_specs,
        out_specs=pl.BlockSpec((1, tq, LANES), lambda bi, pi, qi: (bi, qi, pi)),
        scratch_shapes=scratch,
        compiler_params=pltpu.CompilerParams(
            dimension_semantics=("arbitrary", "arbitrary", "arbitrary"), vmem_limit_bytes=VMEM_LIMIT),
        name="attn_" + mode + ("_" + extra if extra else ""),
    )(*args)


def _out_kernel(*refs, n_o):
    o_refs = refs[:n_o]
    gate_ref, x_ref, w_ref, g_ref, out_ref = refs[n_o:]
    gate = _silu(gate_ref[...].astype(F32))
    y = None
    col = 0
    for o_ref in o_refs:
        wdt = o_ref.shape[1]
        og = (o_ref[...].astype(F32) * gate[:, col:col + wdt]).astype(BF16)
        part = _dot(og, w_ref[col:col + wdt, :])
        y = part if y is None else y + part
        col += wdt
    out_ref[...] = x_ref[...] + _rms(y, g_ref[...])


def _out_stage(os, gate, x, w, g, tm):
    n, d = x.shape
    os = [o.reshape(n, o.shape[-1]) for o in os]
    in_specs = [pl.BlockSpec((tm, o.shape[1]), lambda i: (i, 0)) for o in os]
    in_specs += [pl.BlockSpec((tm, gate.shape[1]), lambda i: (i, 0)),
                 pl.BlockSpec((tm, d), lambda i: (i, 0)),
                 pl.BlockSpec(w.shape, lambda i: (0, 0)),
                 pl.BlockSpec((1, d), lambda i: (0, 0))]
    return pl.pallas_call(
        functools.partial(_out_kernel, n_o=len(os)),
        out_shape=jax.ShapeDtypeStruct((n, d), F32),
        grid=(n // tm,),
        in_specs=in_specs,
        out_specs=pl.BlockSpec((tm, d), lambda i: (i, 0)),
        compiler_params=pltpu.CompilerParams(
            dimension_semantics=("arbitrary",), vmem_limit_bytes=VMEM_LIMIT),
        name="out_stage",
    )(*os, gate, x, w, g.reshape(1, d))


def kernel(x, positions, l0_pre_g, l0_post_g, l0_w_in, l0_q_a_g, l0_w_q_b, l0_kv_a_g, l0_w_kv_b, l0_w_out,
           l1_pre_g, l1_post_g, l1_w_in, l1_b_f, l1_w_out):
    b, s, d = x.shape
    tm = min(ROW_TILE, s)
    tq = min(Q_TILE, s)
    dch = min(DIAG_CHUNK, tq // 2)
    ch = min(SOFTMAX_CHUNK, tq // 2)
    assert tq % (2 * ch) == 0 and ch % dch == 0 and s % tq == 0
    sb_scale = HEAD_DIM ** -0.5
    mla_scale = (MLA_NOPE + MLA_ROPE) ** -0.5

    w = l0_w_in
    c0 = 4 * SB_WIDTH
    c1 = c0 + MLA_Q_LORA
    c2 = c1 + MLA_KV_LORA
    c3 = c2 + MLA_ROPE
    half = MLA_ROPE // 2
    wkr = w[:, c2:c3]
    wkr_rot = jnp.concatenate([-wkr[:, half:], wkr[:, :half]], axis=1)
    lane_pad = jnp.zeros((d, LANES - MLA_ROPE), F32)
    wn0 = jnp.concatenate([w[:, 512:1024], w[:, 1536:2048], w[:, c3:], w[:, c0:c1], w[:, c1:c2],
                           wkr, lane_pad, wkr_rot, lane_pad], axis=1).astype(BF16)
    wt0 = jnp.concatenate([w[:, 0:512] * sb_scale, w[:, 1024:1536]], axis=1).T.astype(BF16)
    wqbt = (l0_w_q_b * (mla_scale * LOG2E)).T.astype(BF16)
    wkv = l0_w_kv_b.reshape(MLA_KV_LORA, MLA_HEADS, MLA_NOPE + MLA_V)
    wk = wkv[:, :, :MLA_NOPE].reshape(MLA_KV_LORA, -1).astype(BF16)
    wvt = wkv[:, :, MLA_NOPE:].reshape(MLA_KV_LORA, -1).T.astype(BF16)
    invf = ROPE_THETA ** (-jnp.arange(0, MLA_ROPE, 2, dtype=F32) / MLA_ROPE)

    qt_sb, k_sb, vt_sb, gate0, qt_mla, k_mla, kx_mla, vt_mla = _l0_in(
        x, positions, l0_pre_g, wn0, wt0, l0_q_a_g, wqbt, l0_kv_a_g, wk, wvt, invf, tm)
    o_sb = _attention(qt_sb, k_sb, vt_sb, None, mode="stick", extra=None, tq=tq, ch=ch, dch=dch)
    o_mla = _attention(qt_mla, k_mla, vt_mla, kx_mla, mode="softmax", extra="rows", tq=tq, ch=ch, dch=dch)
    x1 = _out_stage([o_sb, o_mla], gate0.reshape(b * s, -1), x.reshape(b * s, d),
                    l0_w_out.astype(BF16), l0_post_g, tm)

    w1 = l1_w_in
    f_pad = jnp.zeros((d, LANES - FOX_HEADS), F32)
    wn1 = jnp.concatenate([w1[:, 1024:2048], w1[:, 3072:4096], w1[:, 4096:], f_pad], axis=1).astype(BF16)
    wt1 = jnp.concatenate([w1[:, 0:1024] * (sb_scale * LOG2E), w1[:, 2048:3072]], axis=1).T.astype(BF16)
    qt1, k1, vt1, gate1, kx1 = _l1_in(x1.reshape(b, s, d), l1_pre_g, wn1, wt1, l1_b_f, tm)
    o1 = _attention(qt1, k1, vt1, kx1, mode="softmax", extra="bias", tq=tq, ch=ch, dch=dch)
    out = _out_stage([o1], gate1.reshape(b * s, -1), x1, l1_w_out.astype(BF16), l1_post_g, tm)
    return out.reshape(b, s, d)
```

```python
import functools
import math

import jax
import jax.numpy as jnp
from jax import lax
from jax.experimental import pallas as pl
from jax.experimental.pallas import tpu as pltpu

D_MODEL = 1024
RMS_EPS = 1e-6
HEAD_DIM = 64
SB_HEADS = 8
SB_WIDTH = SB_HEADS * HEAD_DIM
MLA_HEADS = 8
MLA_NOPE = 64
MLA_ROPE = 32
MLA_V = 64
MLA_Q_LORA = 384
MLA_KV_LORA = 256
MLA_WIDTH = MLA_HEADS * MLA_V
ROPE_THETA = 10000.0
FOX_HEADS = 16
FOX_WIDTH = FOX_HEADS * HEAD_DIM

LANES = 128
ROW_TILE = 512
Q_TILE = 1024
SOFTMAX_CHUNK = 512
DIAG_CHUNK = 256
SUM_ROWS = 16
MASK_VALUE = -1e30
STICK_DEAD = -104.0
SCORE_DEAD = -152.0
LOG2E = math.log2(math.e)
VMEM_LIMIT = 48 * 1024 * 1024
F32 = jnp.float32
BF16 = jnp.bfloat16


def _rms(xf, g):
    var = jnp.mean(xf * xf, axis=-1, keepdims=True)
    return xf * lax.rsqrt(var + RMS_EPS) * g


def _dot(a, b):
    return jnp.dot(a, b, preferred_element_type=F32)


def _dot_nt(a, b):
    return lax.dot_general(a, b, (((1,), (1,)), ((), ())), preferred_element_type=F32)


def _silu(g):
    return g / (1.0 + jnp.exp(-g))


def _split3(v):
    a1 = v.astype(BF16)
    r1 = v - a1.astype(F32)
    a2 = r1.astype(BF16)
    a3 = (r1 - a2.astype(F32)).astype(BF16)
    return a1, a2, a3


def _write_padded_heads(ref, rows_t, heads, width):
    tm = rows_t.shape[1]
    zeros = jnp.zeros((HEAD_DIM, tm), BF16)
    for hh in range(heads):
        off = HEAD_DIM * (hh % 2)
        ref[0, hh, off:off + HEAD_DIM, :] = rows_t[hh * width:hh * width + HEAD_DIM].astype(BF16)
        ref[0, hh, HEAD_DIM - off:2 * HEAD_DIM - off, :] = zeros


def _l0_in_kernel(x_ref, posr_ref, posc_ref, preg_ref, wn_ref, wt_ref, qag_ref, wqbt_ref,
                  kvag_ref, wk_ref, wvt_ref, invfr_ref, invfc_ref,
                  qtsb_ref, ksb_ref, vtsb_ref, gate_ref, qtmla_ref, kmla_ref, kxmla_ref, vtmla_ref):
    x = x_ref[0]
    h = _rms(x, preg_ref[...]).astype(BF16)

    def nat(a, b):
        return _dot(h, wn_ref[:, a:b])

    ksb_ref[0] = nat(0, 512).astype(BF16)
    gate_ref[0] = nat(512, 1536).astype(BF16)
    qa = nat(1536, 1920)
    ckv = nat(1920, 2176)
    kr = nat(2176, 2304)
    krr = nat(2304, 2432)

    tr = _dot_nt(wt_ref[...], h)
    _write_padded_heads(qtsb_ref, tr[0:SB_WIDTH], SB_HEADS, HEAD_DIM)
    vtsb_ref[0] = tr[SB_WIDTH:2 * SB_WIDTH].astype(BF16)

    qan = _rms(qa, qag_ref[...]).astype(BF16)
    qt = _dot_nt(wqbt_ref[...], qan)
    angt = invfc_ref[...] * posr_ref[0].astype(F32)
    cost = jnp.cos(angt)
    sint = jnp.sin(angt)
    tm = qt.shape[1]
    qtmla_ref[...] = jnp.zeros(qtmla_ref.shape, BF16)
    per_head = MLA_NOPE + MLA_ROPE
    half = MLA_ROPE // 2
    for hh in range(MLA_HEADS):
        base = hh * per_head
        off = MLA_NOPE * (hh % 2)
        qtmla_ref[0, hh, off:off + MLA_NOPE, :] = qt[base:base + MLA_NOPE].astype(BF16)
        x1 = qt[base + MLA_NOPE:base + MLA_NOPE + half]
        x2 = qt[base + MLA_NOPE + half:base + per_head]
        qtmla_ref[0, hh, LANES:LANES + half, :] = (x1 * cost - x2 * sint).astype(BF16)
        qtmla_ref[0, hh, LANES + half:LANES + MLA_ROPE, :] = (x2 * cost + x1 * sint).astype(BF16)

    ckvn = _rms(ckv, kvag_ref[...]).astype(BF16)
    kmla_ref[0] = _dot(ckvn, wk_ref[...]).astype(BF16)
    vtmla_ref[0] = _dot_nt(wvt_ref[...], ckvn).astype(BF16)
    ang = posc_ref[0].astype(F32) * invfr_ref[...]
    kxmla_ref[0] = (kr * jnp.cos(ang) + krr * jnp.sin(ang)).astype(BF16)


def _l0_in(x, positions, pre_g, wn, wt, qag, wqbt, kvag, wk, wvt, invf, tm):
    b, s, d = x.shape
    nt = s // tm
    posr = positions.reshape(b, 1, s)
    posc = positions.reshape(b, s, 1)
    invfr = jnp.tile(invf, LANES // invf.shape[0]).reshape(1, LANES)
    invfc = invf.reshape(-1, 1)
    const = lambda shape: pl.BlockSpec(shape, lambda bi, ti: (0,) * len(shape))
    row = lambda w: pl.BlockSpec((1, tm, w), lambda bi, ti: (bi, ti, 0))
    colt = lambda r: pl.BlockSpec((1, r, tm), lambda bi, ti: (bi, 0, ti))
    headt = lambda hn, r: pl.BlockSpec((1, hn, r, tm), lambda bi, ti: (bi, 0, 0, ti))
    out_shape = (
        jax.ShapeDtypeStruct((b, SB_HEADS, LANES, s), BF16),
        jax.ShapeDtypeStruct((b, s, SB_WIDTH), BF16),
        jax.ShapeDtypeStruct((b, SB_WIDTH, s), BF16),
        jax.ShapeDtypeStruct((b, s, SB_WIDTH + MLA_WIDTH), BF16),
        jax.ShapeDtypeStruct((b, MLA_HEADS, 2 * LANES, s), BF16),
        jax.ShapeDtypeStruct((b, s, MLA_WIDTH), BF16),
        jax.ShapeDtypeStruct((b, s, LANES), BF16),
        jax.ShapeDtypeStruct((b, MLA_WIDTH, s), BF16),
    )
    return pl.pallas_call(
        _l0_in_kernel,
        out_shape=out_shape,
        grid=(b, nt),
        in_specs=[
            row(d),
            pl.BlockSpec((1, 1, tm), lambda bi, ti: (bi, 0, ti)),
            pl.BlockSpec((1, tm, 1), lambda bi, ti: (bi, ti, 0)),
            const((1, d)), const(wn.shape), const(wt.shape), const((1, MLA_Q_LORA)), const(wqbt.shape),
            const((1, MLA_KV_LORA)), const(wk.shape), const(wvt.shape),
            const((1, LANES)), const((MLA_ROPE // 2, 1)),
        ],
        out_specs=(headt(SB_HEADS, LANES), row(SB_WIDTH), colt(SB_WIDTH), row(SB_WIDTH + MLA_WIDTH),
                   headt(MLA_HEADS, 2 * LANES), row(MLA_WIDTH), row(LANES), colt(MLA_WIDTH)),
        compiler_params=pltpu.CompilerParams(
            dimension_semantics=("arbitrary", "arbitrary"), vmem_limit_bytes=VMEM_LIMIT),
        name="l0_in",
    )(x, posr, posc, pre_g.reshape(1, d), wn, wt, qag.reshape(1, -1), wqbt,
      kvag.reshape(1, -1), wk, wvt, invfr, invfc)


def _l1_in_kernel(x_ref, preg_ref, wn_ref, wt_ref, bf_ref, tri_ref, place_ref, group_ref,
                  qt_ref, k_ref, vt_ref, gate_ref, kx_ref, bounds_ref, carry_ref):
    @pl.when(pl.program_id(1) == 0)
    def _():
        carry_ref[...] = jnp.zeros(carry_ref.shape, F32)

    x = x_ref[0]
    h = _rms(x, preg_ref[...]).astype(BF16)
    kb = _dot(h, wn_ref[:, 0:1024]).astype(BF16)
    k_ref[0] = kb
    gate_ref[0] = _dot(h, wn_ref[:, 1024:2048]).astype(BF16)
    f = _dot(h, wn_ref[:, 2048:2176])[:, 0:FOX_HEADS] + bf_ref[...]

    tr = _dot_nt(wt_ref[...], h)
    _write_padded_heads(qt_ref, tr[0:FOX_WIDTH], FOX_HEADS, HEAD_DIM)
    vt_ref[0] = tr[FOX_WIDTH:2 * FOX_WIDTH].astype(BF16)

    logf = jnp.minimum(f, 0.0) - jnp.log1p(jnp.exp(-jnp.abs(f)))
    tri = tri_ref[...]
    c = carry_ref[0:1, 0:FOX_HEADS]
    for term in _split3(logf):
        c = c + _dot(tri, term)
    tm = c.shape[0]
    carry_ref[0:1, 0:FOX_HEADS] = c[tm - 1:tm, :]
    slab = jnp.zeros((tm, LANES), F32)
    for t, term in enumerate(_split3(-LOG2E * c)):
        slab = slab + _dot(term, place_ref[t])
    kx_ref[0] = slab.astype(BF16)
    kf = kb.astype(F32)
    top = jnp.broadcast_to(jnp.max(kf * kf, axis=0, keepdims=True), (8, FOX_WIDTH)).astype(BF16)
    ksq = _dot(top, group_ref[...])
    bounds_ref[0, 0] = jnp.concatenate(
        [ksq[0:1, :], jnp.max(slab, axis=0, keepdims=True), jnp.zeros((6, LANES), F32)], axis=0)


def _l1_in(x, pre_g, wn, wt, bf, tm):
    b, s, d = x.shape
    nt = s // tm
    tri = (lax.broadcasted_iota(jnp.int32, (tm, tm), 1)
           <= lax.broadcasted_iota(jnp.int32, (tm, tm), 0)).astype(BF16)
    hh = lax.broadcasted_iota(jnp.int32, (3, FOX_HEADS, LANES), 1)
    tt = lax.broadcasted_iota(jnp.int32, (3, FOX_HEADS, LANES), 0)
    ll = lax.broadcasted_iota(jnp.int32, (3, FOX_HEADS, LANES), 2)
    place = (ll == tt * FOX_HEADS + hh).astype(BF16)
    group = (lax.broadcasted_iota(jnp.int32, (FOX_WIDTH, LANES), 0) // HEAD_DIM
             == lax.broadcasted_iota(jnp.int32, (FOX_WIDTH, LANES), 1)).astype(BF16)
    const = lambda shape: pl.BlockSpec(shape, lambda bi, ti: (0,) * len(shape))
    row = lambda w: pl.BlockSpec((1, tm, w), lambda bi, ti: (bi, ti, 0))
    out_shape = (
        jax.ShapeDtypeStruct((b, FOX_HEADS, LANES, s), BF16),
        jax.ShapeDtypeStruct((b, s, FOX_WIDTH), BF16),
        jax.ShapeDtypeStruct((b, FOX_WIDTH, s), BF16),
        jax.ShapeDtypeStruct((b, s, FOX_WIDTH), BF16),
        jax.ShapeDtypeStruct((b, s, LANES), BF16),
        jax.ShapeDtypeStruct((b, nt, 8, LANES), F32),
    )
    return pl.pallas_call(
        _l1_in_kernel,
        out_shape=out_shape,
        grid=(b, nt),
        in_specs=[row(d), const((1, d)), const(wn.shape), const(wt.shape),
                  const((1, FOX_HEADS)), const((tm, tm)), const(place.shape), const(group.shape)],
        out_specs=(pl.BlockSpec((1, FOX_HEADS, LANES, tm), lambda bi, ti: (bi, 0, 0, ti)),
                   row(FOX_WIDTH),
                   pl.BlockSpec((1, FOX_WIDTH, tm), lambda bi, ti: (bi, 0, ti)),
                   row(FOX_WIDTH), row(LANES),
                   pl.BlockSpec((1, 1, 8, LANES), lambda bi, ti: (bi, ti, 0, 0))),
        scratch_shapes=[pltpu.VMEM((8, LANES), F32)],
        compiler_params=pltpu.CompilerParams(
            dimension_semantics=("arbitrary", "arbitrary"), vmem_limit_bytes=VMEM_LIMIT),
        name="l1_in",
    )(x, pre_g.reshape(1, d), wn, wt, bf.reshape(1, -1), tri, place, group)


def _attn_kernel(*refs, mode, extra, tq, ch, dch, key_tile):
    refs = list(refs)
    qt_ref = refs.pop(0)
    k_ref = refs.pop(0)
    kx_ref = refs.pop(0) if extra else None
    tab_ref = refs.pop(0) if extra == "bias" else None
    vt_ref = refs.pop(0)
    u_ref = refs.pop(0) if mode == "stick" else None
    o_ref, acc_ref, stat_ref, sbuf_ref = refs[:4]

    pi = pl.program_id(1)
    q0 = pl.program_id(2) * tq
    nd = tq // dch

    qts = []
    for hh in range(2):
        qt = qt_ref[0, hh]
        if extra == "bias":
            hg = 2 * pi + hh
            r = lax.broadcasted_iota(jnp.int32, (LANES, tq), 0)
            hit = (r == hg) | (r == hg + FOX_HEADS) | (r == hg + 2 * FOX_HEADS)
            qt = jnp.concatenate([qt, jnp.where(hit, 1.0, 0.0).astype(BF16)], axis=0)
        qts.append(qt)

    def k_chunk(start, n):
        k = k_ref[0, pl.ds(start, n), :]
        if extra:
            k = jnp.concatenate([k, kx_ref[0, pl.ds(start, n), :]], axis=1)
        return k

    def v_chunk(hh, start, n):
        return vt_ref[0, hh * HEAD_DIM:(hh + 1) * HEAD_DIM, pl.ds(start, n)]

    acc_ref[...] = jnp.zeros(acc_ref.shape, F32)
    rr = lax.broadcasted_iota(jnp.int32, (dch, dch), 0)
    cc = lax.broadcasted_iota(jnp.int32, (dch, dch), 1)

    def diag_ranges(c):
        off = c * dch
        out = [(off, off + dch, True)]
        if off + dch < tq:
            out.append((off + dch, tq, False))
        return out

    if mode == "softmax":
        nfull = q0 // ch
        per = ch // dch
        nb = tq // ch

        def with_ones(v):
            return jnp.concatenate([v, jnp.ones((SUM_ROWS, v.shape[1]), BF16)], axis=0)

        def produce(hh, slot, k):
            s = _dot(k, qts[hh])
            sbuf_ref[slot, hh] = s
            stat_ref[4 * hh + 2 + slot:4 * hh + 3 + slot, :] = jnp.max(s, axis=0, keepdims=True)

        for g in range(nb):
            kb = k_chunk(pl.multiple_of(q0 + g * ch, ch), ch)
            for hh in range(2):
                sbuf_ref[2 + g, hh, :, g * ch:] = _dot(kb, qts[hh][:, g * ch:])
        kl = k_chunk(pl.multiple_of(jnp.maximum(nfull - 1, 0) * ch, ch), ch)
        for hh in range(2):
            produce(hh, 1, kl)

        def band_scores(hh, c, c0, c1, tri):
            g, r = divmod(c, per)
            s = sbuf_ref[2 + g, hh, r * dch:(r + 1) * dch, c0:c1]
            return jnp.where(rr <= cc, s, MASK_VALUE) if tri else s

        for hh in range(2):
            bmax = [None] * nd
            for c in range(nd):
                for c0, c1, tri in diag_ranges(c):
                    cm = jnp.max(band_scores(hh, c, c0, c1, tri), axis=0, keepdims=True)
                    for blk in range(c0 // dch, c1 // dch):
                        part = cm[:, blk * dch - c0:(blk + 1) * dch - c0]
                        bmax[blk] = part if bmax[blk] is None else jnp.maximum(bmax[blk], part)
            m_band = jnp.concatenate(bmax, axis=1)
            stat_ref[4 * hh:4 * hh + 1, :] = m_band
            for c in range(nd):
                off = c * dch
                ps = [jnp.exp2(band_scores(hh, c, c0, c1, tri) - m_band[:, c0:c1])
                      for c0, c1, tri in diag_ranges(c)]
                p = ps[0] if len(ps) == 1 else jnp.concatenate(ps, axis=1)
                pv = _dot(with_ones(v_chunk(hh, pl.multiple_of(q0 + off, dch), dch)), p.astype(BF16))
                if c == 0:
                    acc_ref[hh] = pv
                else:
                    acc_ref[hh, :, off:] = acc_ref[hh, :, off:] + pv

        jlo = 0
        if extra == "bias":
            tab_k = tab_ref[0, 0]
            tab_b = tab_ref[0, 1]
            tiles = tab_k.shape[0]
            lane = lax.broadcasted_iota(jnp.int32, tab_k.shape, 1)
            dead = None
            for hh in range(2):
                pick = lane == 2 * pi + hh
                k_hi = jnp.sqrt(jnp.sum(jnp.where(pick, tab_k, 0.0), axis=1, keepdims=True) * 1.02)
                b_top = jnp.sum(jnp.where(pick, tab_b, 0.0), axis=1, keepdims=True)
                b_hi = b_top + jnp.abs(b_top) * (2.0 ** -7)
                qf = qts[hh][0:LANES, :].astype(F32)
                q_hi = jnp.sqrt(jnp.max(jnp.sum(qf * qf, axis=0, keepdims=True), axis=1, keepdims=True) * 1.0001)
                m_lo = jnp.min(stat_ref[4 * hh:4 * hh + 1, :], axis=1, keepdims=True)
                gone = k_hi * q_hi + b_hi - m_lo <= SCORE_DEAD
                dead = gone if dead is None else dead & gone
            tile_idx = lax.broadcasted_iota(jnp.int32, (tiles, 1), 0)
            first_live = jnp.min(jnp.where(dead, tiles, tile_idx))
            jlo = jnp.minimum(first_live * (key_tile // ch), nfull)
            jlo = jlo - lax.rem(jlo, 2)

        def half_step(j, slot, prefetch):
            start = pl.multiple_of(j * ch, ch)
            if prefetch:
                kn = k_chunk(pl.multiple_of(start - ch, ch), ch)
            for hh in range(2):
                if prefetch:
                    produce(hh, 1 - slot, kn)
                m_prev = stat_ref[4 * hh:4 * hh + 1, :]
                m_new = jnp.maximum(m_prev, stat_ref[4 * hh + 2 + slot:4 * hh + 3 + slot, :])
                stat_ref[4 * hh:4 * hh + 1, :] = m_new
                p = jnp.exp2(sbuf_ref[slot, hh] - m_new)
                pv = _dot(with_ones(v_chunk(hh, start, ch)), p.astype(BF16))
                acc_ref[hh] = jnp.exp2(m_prev - m_new) * acc_ref[hh] + pv

        def body(t, carry):
            j = nfull - 1 - 2 * t
            half_step(j, 1, True)
            half_step(j - 1, 0, True)
            return carry

        npair = (nfull - jlo) // 2
        lax.fori_loop(0, npair - 1, body, 0)

        @pl.when(npair > 0)
        def _():
            half_step(jlo + 1, 1, True)
            half_step(jlo, 0, False)

        o_t = jnp.concatenate([acc_ref[hh, 0:HEAD_DIM, :] / acc_ref[hh, HEAD_DIM:HEAD_DIM + 1, :]
                               for hh in range(2)], axis=0)
    else:
        hl_ref = refs[4]
        nfull = q0 // dch
        stat_ref[...] = jnp.zeros(stat_ref.shape, F32)

        def mask_first(x, fill):
            first = jnp.where(rr < cc, x[:, 0:dch], fill)
            return first if x.shape[1] == dch else jnp.concatenate([first, x[:, dch:]], axis=1)

        def produce(hh, slot, k, off, diag):
            z = _dot(k, qts[hh][:, off:])
            sp = jnp.maximum(z, 0.0) + jnp.log(1.0 + jnp.exp2(jnp.abs(z) * -LOG2E))
            if diag:
                sp = mask_first(sp, 0.0)
            sbuf_ref[slot, hh, :, off:] = z - sp
            hl_ref[slot, hh, :, off:] = sp.astype(BF16)
            stat_ref[2 + 2 * slot + hh:3 + 2 * slot + hh, off:] = sp[0:1, :]

        def suffix_sums(hh, slot, off):
            u = u_ref[...]
            return _dot(u, hl_ref[slot, hh, :, off:])

        def consume(hh, slot, remain, start, off, diag):
            carry = stat_ref[hh:hh + 1, off:]
            first = stat_ref[2 + 2 * slot + hh:3 + 2 * slot + hh, off:]
            stat_ref[hh:hh + 1, off:] = carry + remain[0:1, :] - first
            w = jnp.exp(sbuf_ref[slot, hh, :, off:] + remain + carry)
            if diag:
                w = mask_first(w, 0.0)
            acc_ref[hh, :, off:] = acc_ref[hh, :, off:] + _dot(v_chunk(hh, start, dch), w.astype(BF16))

        for c in reversed(range(nd)):
            kc = k_chunk(pl.multiple_of(q0 + c * dch, dch), dch)
            for hh in range(2):
                produce(hh, 2 + c, kc, c * dch, True)
        kl = k_chunk(pl.multiple_of(jnp.maximum(nfull - 1, 0) * dch, dch), dch)
        for hh in range(2):
            produce(hh, 1, kl, 0, False)
        ahead = [suffix_sums(hh, 1 + nd, (nd - 1) * dch) for hh in range(2)]
        for c in reversed(range(nd)):
            remain = ahead
            if c > 0:
                ahead = [suffix_sums(hh, 1 + c, (c - 1) * dch) for hh in range(2)]
            for hh in range(2):
                consume(hh, 2 + c, remain[hh], pl.multiple_of(q0 + c * dch, dch), c * dch, True)

        def half_step(j, slot, prefetch):
            start = pl.multiple_of(j * dch, dch)
            if prefetch:
                kn = k_chunk(pl.multiple_of((j - 1) * dch, dch), dch)
            for hh in range(2):
                remain = suffix_sums(hh, slot, 0)
                if prefetch:
                    produce(hh, 1 - slot, kn, 0, False)
                consume(hh, slot, remain, start, 0, False)

        def stick_left():
            return (jnp.max(stat_ref[0:2, :]) > STICK_DEAD).astype(jnp.int32)

        def pair(state):
            t, _ = state
            j = nfull - 1 - 2 * t
            half_step(j, 1, True)
            half_step(j - 1, 0, True)
            return t + 1, stick_left()

        npair = nfull // 2
        _, alive = lax.while_loop(lambda st: (st[0] < npair - 1) & (st[1] > 0), pair,
                                  (jnp.int32(0), stick_left()))

        @pl.when((nfull > 0) & (alive > 0))
        def _():
            half_step(1, 1, True)
            half_step(0, 0, False)

        o_t = jnp.concatenate([acc_ref[0], acc_ref[1]], axis=0)

    o_ref[0] = o_t.T.astype(o_ref.dtype)


def _attention(qt, k, vt, kx, *, mode, extra, tq, ch, dch, bounds=None):
    b, s, width = k.shape
    heads = qt.shape[1]
    kd = qt.shape[2]
    in_specs = [pl.BlockSpec((1, 2, kd, tq), lambda bi, pi, qi: (bi, pi, 0, qi)),
                pl.BlockSpec((1, s, LANES), lambda bi, pi, qi: (bi, 0, pi))]
    args = [qt, k]
    key_tile = ch
    if extra:
        in_specs.append(pl.BlockSpec((1, s, LANES), lambda bi, pi, qi: (bi, 0, 0)))
        args.append(kx)
    if extra == "bias":
        in_specs.append(pl.BlockSpec((1,) + bounds.shape[1:], lambda bi, pi, qi: (bi, 0, 0, 0)))
        args.append(bounds)
        key_tile = s // bounds.shape[2]
        assert key_tile % ch == 0
    in_specs.append(pl.BlockSpec((1, LANES, s), lambda bi, pi, qi: (bi, pi, 0)))
    args.append(vt)
    acc_rows = HEAD_DIM + (SUM_ROWS if mode == "softmax" else 0)
    nslot = 2 + tq // dch
    stat_rows = 8 * pl.cdiv(2 + 2 * nslot, 8)
    scratch = [pltpu.VMEM((2, acc_rows, tq), F32), pltpu.VMEM((stat_rows, tq), F32)]
    if mode == "softmax":
        scratch.append(pltpu.VMEM((2 + tq // ch, 2, ch, tq), F32))
    else:
        upper = jnp.where(lax.broadcasted_iota(jnp.int32, (dch, dch), 1)
                          > lax.broadcasted_iota(jnp.int32, (dch, dch), 0), -1.0, 0.0).astype(BF16)
        in_specs.append(pl.BlockSpec((dch, dch), lambda bi, pi, qi: (0, 0)))
        args.append(upper)
        scratch += [pltpu.VMEM((nslot, 2, dch, tq), F32),
                    pltpu.VMEM((nslot, 2, dch, tq), BF16)]
    kernel = functools.partial(_attn_kernel, mode=mode, extra=extra, tq=tq, ch=ch, dch=dch, key_tile=key_tile)
    return pl.pallas_call(
        kernel,
        out_shape=jax.ShapeDtypeStruct((b, s, width), BF16),
        grid=(b, heads // 2, s // tq),
        in_specs=in_specs,
        out_specs=pl.BlockSpec((1, tq, LANES), lambda bi, pi, qi: (bi, qi, pi)),
        scratch_shapes=scratch,
        compiler_params=pltpu.CompilerParams(
            dimension_semantics=("arbitrary", "arbitrary", "arbitrary"), vmem_limit_bytes=VMEM_LIMIT),
        name="attn_" + mode + ("_" + extra if extra else ""),
    )(*args)


def _out_kernel(*refs, n_o):
    o_refs = refs[:n_o]
    gate_ref, x_ref, w_ref, g_ref, out_ref = refs[n_o:]
    gate = _silu(gate_ref[...].astype(F32))
    y = None
    col = 0
    for o_ref in o_refs:
        wdt = o_ref.shape[1]
        og = (o_ref[...].astype(F32) * gate[:, col:col + wdt]).astype(BF16)
        part = _dot(og, w_ref[col:col + wdt, :])
        y = part if y is None else y + part
        col += wdt
    out_ref[...] = x_ref[...] + _rms(y, g_ref[...])


def _out_stage(os, gate, x, w, g, tm):
    n, d = x.shape
    os = [o.reshape(n, o.shape[-1]) for o in os]
    in_specs = [pl.BlockSpec((tm, o.shape[1]), lambda i: (i, 0)) for o in os]
    in_specs += [pl.BlockSpec((tm, gate.shape[1]), lambda i: (i, 0)),
                 pl.BlockSpec((tm, d), lambda i: (i, 0)),
                 pl.BlockSpec(w.shape, lambda i: (0, 0)),
                 pl.BlockSpec((1, d), lambda i: (0, 0))]
    return pl.pallas_call(
        functools.partial(_out_kernel, n_o=len(os)),
        out_shape=jax.ShapeDtypeStruct((n, d), F32),
        grid=(n // tm,),
        in_specs=in_specs,
        out_specs=pl.BlockSpec((tm, d), lambda i: (i, 0)),
        compiler_params=pltpu.CompilerParams(
            dimension_semantics=("arbitrary",), vmem_limit_bytes=VMEM_LIMIT),
        name="out_stage",
    )(*os, gate, x, w, g.reshape(1, d))


def kernel(x, positions, l0_pre_g, l0_post_g, l0_w_in, l0_q_a_g, l0_w_q_b, l0_kv_a_g, l0_w_kv_b, l0_w_out,
           l1_pre_g, l1_post_g, l1_w_in, l1_b_f, l1_w_out):
    b, s, d = x.shape
    tm = min(ROW_TILE, s)
    tq = min(Q_TILE, s)
    dch = min(DIAG_CHUNK, tq // 2)
    ch = min(SOFTMAX_CHUNK, tq // 2)
    assert tq % (2 * ch) == 0 and ch % dch == 0 and s % tq == 0
    sb_scale = HEAD_DIM ** -0.5
    mla_scale = (MLA_NOPE + MLA_ROPE) ** -0.5

    w = l0_w_in
    c0 = 4 * SB_WIDTH
    c1 = c0 + MLA_Q_LORA
    c2 = c1 + MLA_KV_LORA
    c3 = c2 + MLA_ROPE
    half = MLA_ROPE // 2
    wkr = w[:, c2:c3]
    wkr_rot = jnp.concatenate([-wkr[:, half:], wkr[:, :half]], axis=1)
    lane_pad = jnp.zeros((d, LANES - MLA_ROPE), F32)
    wn0 = jnp.concatenate([w[:, 512:1024], w[:, 1536:2048], w[:, c3:], w[:, c0:c1], w[:, c1:c2],
                           wkr, lane_pad, wkr_rot, lane_pad], axis=1).astype(BF16)
    wt0 = jnp.concatenate([w[:, 0:512] * sb_scale, w[:, 1024:1536]], axis=1).T.astype(BF16)
    wqbt = (l0_w_q_b * (mla_scale * LOG2E)).T.astype(BF16)
    wkv = l0_w_kv_b.reshape(MLA_KV_LORA, MLA_HEADS, MLA_NOPE + MLA_V)
    wk = wkv[:, :, :MLA_NOPE].reshape(MLA_KV_LORA, -1).astype(BF16)
    wvt = wkv[:, :, MLA_NOPE:].reshape(MLA_KV_LORA, -1).T.astype(BF16)
    invf = ROPE_THETA ** (-jnp.arange(0, MLA_ROPE, 2, dtype=F32) / MLA_ROPE)

    qt_sb, k_sb, vt_sb, gate0, qt_mla, k_mla, kx_mla, vt_mla = _l0_in(
        x, positions, l0_pre_g, wn0, wt0, l0_q_a_g, wqbt, l0_kv_a_g, wk, wvt, invf, tm)
    o_sb = _attention(qt_sb, k_sb, vt_sb, None, mode="stick", extra=None, tq=tq, ch=ch, dch=dch)
    o_mla = _attention(qt_mla, k_mla, vt_mla, kx_mla, mode="softmax", extra="rows", tq=tq, ch=ch, dch=dch)
    x1 = _out_stage([o_sb, o_mla], gate0.reshape(b * s, -1), x.reshape(b * s, d),
                    l0_w_out.astype(BF16), l0_post_g, tm)

    w1 = l1_w_in
    f_pad = jnp.zeros((d, LANES - FOX_HEADS), F32)
    wn1 = jnp.concatenate([w1[:, 1024:2048], w1[:, 3072:4096], w1[:, 4096:], f_pad], axis=1).astype(BF16)
    wt1 = jnp.concatenate([w1[:, 0:1024] * (sb_scale * LOG2E), w1[:, 2048:3072]], axis=1).T.astype(BF16)
    qt1, k1, vt1, gate1, kx1, tile_bounds = _l1_in(x1.reshape(b, s, d), l1_pre_g, wn1, wt1, l1_b_f, tm)
    bounds = tile_bounds[:, :, 0:2, :].transpose(0, 2, 1, 3)
    o1 = _attention(qt1, k1, vt1, kx1, mode="softmax", extra="bias", tq=tq, ch=ch, dch=dch, bounds=bounds)
    out = _out_stage([o1], gate1.reshape(b * s, -1), x1, l1_w_out.astype(BF16), l1_post_g, tm)
    return out.reshape(b, s, d)
```

```python
import functools
import math

import jax
import jax.numpy as jnp
from jax import lax
from jax.experimental import pallas as pl
from jax.experimental.pallas import tpu as pltpu

D_MODEL = 1024
RMS_EPS = 1e-6
HEAD_DIM = 64
SB_HEADS = 8
SB_WIDTH = SB_HEADS * HEAD_DIM
MLA_HEADS = 8
MLA_NOPE = 64
MLA_ROPE = 32
MLA_V = 64
MLA_Q_LORA = 384
MLA_KV_LORA = 256
MLA_WIDTH = MLA_HEADS * MLA_V
ROPE_THETA = 10000.0
FOX_HEADS = 16
FOX_WIDTH = FOX_HEADS * HEAD_DIM

LANES = 128
ROW_TILE = 512
Q_TILE = 1024
SOFTMAX_CHUNK = 512
DIAG_CHUNK = 256
SUM_ROWS = 16
MASK_VALUE = -1e30
STICK_DEAD = -104.0
SCORE_DEAD = -152.0
LOG2E = math.log2(math.e)
VMEM_LIMIT = 48 * 1024 * 1024
F32 = jnp.float32
BF16 = jnp.bfloat16


def _rms(xf, g):
    var = jnp.mean(xf * xf, axis=-1, keepdims=True)
    return xf * lax.rsqrt(var + RMS_EPS) * g


def _dot(a, b):
    return jnp.dot(a, b, preferred_element_type=F32)


def _dot_nt(a, b):
    return lax.dot_general(a, b, (((1,), (1,)), ((), ())), preferred_element_type=F32)


def _silu(g):
    return g / (1.0 + jnp.exp(-g))


def _split3(v):
    a1 = v.astype(BF16)
    r1 = v - a1.astype(F32)
    a2 = r1.astype(BF16)
    a3 = (r1 - a2.astype(F32)).astype(BF16)
    return a1, a2, a3


def _write_padded_heads(ref, rows_t, heads, width):
    tm = rows_t.shape[1]
    zeros = jnp.zeros((HEAD_DIM, tm), BF16)
    for hh in range(heads):
        off = HEAD_DIM * (hh % 2)
        ref[0, hh, off:off + HEAD_DIM, :] = rows_t[hh * width:hh * width + HEAD_DIM].astype(BF16)
        ref[0, hh, HEAD_DIM - off:2 * HEAD_DIM - off, :] = zeros


def _l0_in_kernel(x_ref, posr_ref, posc_ref, preg_ref, wn_ref, wt_ref, qag_ref, wqbt_ref,
                  kvag_ref, wk_ref, wvt_ref, invfr_ref, invfc_ref,
                  qtsb_ref, ksb_ref, vtsb_ref, gate_ref, qtmla_ref, kmla_ref, kxmla_ref, vtmla_ref):
    x = x_ref[0]
    h = _rms(x, preg_ref[...]).astype(BF16)

    def nat(a, b):
        return _dot(h, wn_ref[:, a:b])

    ksb_ref[0] = nat(0, 512).astype(BF16)
    gate_ref[0] = nat(512, 1536).astype(BF16)
    qa = nat(1536, 1920)
    ckv = nat(1920, 2176)
    kr = nat(2176, 2304)
    krr = nat(2304, 2432)

    tr = _dot_nt(wt_ref[...], h)
    _write_padded_heads(qtsb_ref, tr[0:SB_WIDTH], SB_HEADS, HEAD_DIM)
    vtsb_ref[0] = tr[SB_WIDTH:2 * SB_WIDTH].astype(BF16)

    qan = _rms(qa, qag_ref[...]).astype(BF16)
    qt = _dot_nt(wqbt_ref[...], qan)
    angt = invfc_ref[...] * posr_ref[0].astype(F32)
    cost = jnp.cos(angt)
    sint = jnp.sin(angt)
    tm = qt.shape[1]
    qtmla_ref[...] = jnp.zeros(qtmla_ref.shape, BF16)
    per_head = MLA_NOPE + MLA_ROPE
    half = MLA_ROPE // 2
    for hh in range(MLA_HEADS):
        base = hh * per_head
        off = MLA_NOPE * (hh % 2)
        qtmla_ref[0, hh, off:off + MLA_NOPE, :] = qt[base:base + MLA_NOPE].astype(BF16)
        x1 = qt[base + MLA_NOPE:base + MLA_NOPE + half]
        x2 = qt[base + MLA_NOPE + half:base + per_head]
        qtmla_ref[0, hh, LANES:LANES + half, :] = (x1 * cost - x2 * sint).astype(BF16)
        qtmla_ref[0, hh, LANES + half:LANES + MLA_ROPE, :] = (x2 * cost + x1 * sint).astype(BF16)

    ckvn = _rms(ckv, kvag_ref[...]).astype(BF16)
    kmla_ref[0] = _dot(ckvn, wk_ref[...]).astype(BF16)
    vtmla_ref[0] = _dot_nt(wvt_ref[...], ckvn).astype(BF16)
    ang = posc_ref[0].astype(F32) * invfr_ref[...]
    kxmla_ref[0] = (kr * jnp.cos(ang) + krr * jnp.sin(ang)).astype(BF16)


def _l0_in(x, positions, pre_g, wn, wt, qag, wqbt, kvag, wk, wvt, invf, tm):
    b, s, d = x.shape
    nt = s // tm
    posr = positions.reshape(b, 1, s)
    posc = positions.reshape(b, s, 1)
    invfr = jnp.tile(invf, LANES // invf.shape[0]).reshape(1, LANES)
    invfc = invf.reshape(-1, 1)
    const = lambda shape: pl.BlockSpec(shape, lambda bi, ti: (0,) * len(shape))
    row = lambda w: pl.BlockSpec((1, tm, w), lambda bi, ti: (bi, ti, 0))
    colt = lambda r: pl.BlockSpec((1, r, tm), lambda bi, ti: (bi, 0, ti))
    headt = lambda hn, r: pl.BlockSpec((1, hn, r, tm), lambda bi, ti: (bi, 0, 0, ti))
    out_shape = (
        jax.ShapeDtypeStruct((b, SB_HEADS, LANES, s), BF16),
        jax.ShapeDtypeStruct((b, s, SB_WIDTH), BF16),
        jax.ShapeDtypeStruct((b, SB_WIDTH, s), BF16),
        jax.ShapeDtypeStruct((b, s, SB_WIDTH + MLA_WIDTH), BF16),
        jax.ShapeDtypeStruct((b, MLA_HEADS, 2 * LANES, s), BF16),
        jax.ShapeDtypeStruct((b, s, MLA_WIDTH), BF16),
        jax.ShapeDtypeStruct((b, s, LANES), BF16),
        jax.ShapeDtypeStruct((b, MLA_WIDTH, s), BF16),
    )
    return pl.pallas_call(
        _l0_in_kernel,
        out_shape=out_shape,
        grid=(b, nt),
        in_specs=[
            row(d),
            pl.BlockSpec((1, 1, tm), lambda bi, ti: (bi, 0, ti)),
            pl.BlockSpec((1, tm, 1), lambda bi, ti: (bi, ti, 0)),
            const((1, d)), const(wn.shape), const(wt.shape), const((1, MLA_Q_LORA)), const(wqbt.shape),
            const((1, MLA_KV_LORA)), const(wk.shape), const(wvt.shape),
            const((1, LANES)), const((MLA_ROPE // 2, 1)),
        ],
        out_specs=(headt(SB_HEADS, LANES), row(SB_WIDTH), colt(SB_WIDTH), row(SB_WIDTH + MLA_WIDTH),
                   headt(MLA_HEADS, 2 * LANES), row(MLA_WIDTH), row(LANES), colt(MLA_WIDTH)),
        compiler_params=pltpu.CompilerParams(
            dimension_semantics=("arbitrary", "arbitrary"), vmem_limit_bytes=VMEM_LIMIT),
        name="l0_in",
    )(x, posr, posc, pre_g.reshape(1, d), wn, wt, qag.reshape(1, -1), wqbt,
      kvag.reshape(1, -1), wk, wvt, invfr, invfc)


def _l1_in_kernel(x_ref, preg_ref, wn_ref, wt_ref, bf_ref, tri_ref, place_ref, group_ref,
                  qt_ref, k_ref, vt_ref, gate_ref, kx_ref, bounds_ref, carry_ref):
    @pl.when(pl.program_id(1) == 0)
    def _():
        carry_ref[...] = jnp.zeros(carry_ref.shape, F32)

    x = x_ref[0]
    h = _rms(x, preg_ref[...]).astype(BF16)
    kb = _dot(h, wn_ref[:, 0:1024]).astype(BF16)
    k_ref[0] = kb
    gate_ref[0] = _dot(h, wn_ref[:, 1024:2048]).astype(BF16)
    f = _dot(h, wn_ref[:, 2048:2176])[:, 0:FOX_HEADS] + bf_ref[...]

    tr = _dot_nt(wt_ref[...], h)
    _write_padded_heads(qt_ref, tr[0:FOX_WIDTH], FOX_HEADS, HEAD_DIM)
    vt_ref[0] = tr[FOX_WIDTH:2 * FOX_WIDTH].astype(BF16)

    logf = jnp.minimum(f, 0.0) - jnp.log1p(jnp.exp(-jnp.abs(f)))
    tri = tri_ref[...]
    c = carry_ref[0:1, 0:FOX_HEADS]
    for term in _split3(logf):
        c = c + _dot(tri, term)
    tm = c.shape[0]
    carry_ref[0:1, 0:FOX_HEADS] = c[tm - 1:tm, :]
    slab = jnp.zeros((tm, LANES), F32)
    for t, term in enumerate(_split3(-LOG2E * c)):
        slab = slab + _dot(term, place_ref[t])
    kx_ref[0] = slab.astype(BF16)
    kf = kb.astype(F32)
    top = jnp.broadcast_to(jnp.max(kf * kf, axis=0, keepdims=True), (8, FOX_WIDTH)).astype(BF16)
    ksq = _dot(top, group_ref[...])
    bounds_ref[0, 0] = jnp.concatenate(
        [ksq[0:1, :], jnp.max(slab, axis=0, keepdims=True), jnp.zeros((6, LANES), F32)], axis=0)


def _l1_in(x, pre_g, wn, wt, bf, tm):
    b, s, d = x.shape
    nt = s // tm
    tri = (lax.broadcasted_iota(jnp.int32, (tm, tm), 1)
           <= lax.broadcasted_iota(jnp.int32, (tm, tm), 0)).astype(BF16)
    hh = lax.broadcasted_iota(jnp.int32, (3, FOX_HEADS, LANES), 1)
    tt = lax.broadcasted_iota(jnp.int32, (3, FOX_HEADS, LANES), 0)
    ll = lax.broadcasted_iota(jnp.int32, (3, FOX_HEADS, LANES), 2)
    place = (ll == tt * FOX_HEADS + hh).astype(BF16)
    group = (lax.broadcasted_iota(jnp.int32, (FOX_WIDTH, LANES), 0) // HEAD_DIM
             == lax.broadcasted_iota(jnp.int32, (FOX_WIDTH, LANES), 1)).astype(BF16)
    const = lambda shape: pl.BlockSpec(shape, lambda bi, ti: (0,) * len(shape))
    row = lambda w: pl.BlockSpec((1, tm, w), lambda bi, ti: (bi, ti, 0))
    out_shape = (
        jax.ShapeDtypeStruct((b, FOX_HEADS, LANES, s), BF16),
        jax.ShapeDtypeStruct((b, s, FOX_WIDTH), BF16),
        jax.ShapeDtypeStruct((b, FOX_WIDTH, s), BF16),
        jax.ShapeDtypeStruct((b, s, FOX_WIDTH), BF16),
        jax.ShapeDtypeStruct((b, s, LANES), BF16),
        jax.ShapeDtypeStruct((b, nt, 8, LANES), F32),
    )
    return pl.pallas_call(
        _l1_in_kernel,
        out_shape=out_shape,
        grid=(b, nt),
        in_specs=[row(d), const((1, d)), const(wn.shape), const(wt.shape),
                  const((1, FOX_HEADS)), const((tm, tm)), const(place.shape), const(group.shape)],
        out_specs=(pl.BlockSpec((1, FOX_HEADS, LANES, tm), lambda bi, ti: (bi, 0, 0, ti)),
                   row(FOX_WIDTH),
                   pl.BlockSpec((1, FOX_WIDTH, tm), lambda bi, ti: (bi, 0, ti)),
                   row(FOX_WIDTH), row(LANES),
                   pl.BlockSpec((1, 1, 8, LANES), lambda bi, ti: (bi, ti, 0, 0))),
        scratch_shapes=[pltpu.VMEM((8, LANES), F32)],
        compiler_params=pltpu.CompilerParams(
            dimension_semantics=("arbitrary", "arbitrary"), vmem_limit_bytes=VMEM_LIMIT),
        name="l1_in",
    )(x, pre_g.reshape(1, d), wn, wt, bf.reshape(1, -1), tri, place, group)


def _attn_kernel(*refs, mode, extra, tq, ch, dch, key_tile):
    refs = list(refs)
    qt_ref = refs.pop(0)
    k_ref = refs.pop(0)
    kx_ref = refs.pop(0) if extra else None
    tab_ref = refs.pop(0) if extra == "bias" else None
    vt_ref = refs.pop(0)
    u_ref = refs.pop(0) if mode == "stick" else None
    o_ref, acc_ref, stat_ref, sbuf_ref = refs[:4]

    pi = pl.program_id(1)
    q0 = pl.program_id(2) * tq
    nd = tq // dch

    qts = []
    for hh in range(2):
        qt = qt_ref[0, hh]
        if extra == "bias":
            hg = 2 * pi + hh
            r = lax.broadcasted_iota(jnp.int32, (LANES, tq), 0)
            hit = (r == hg) | (r == hg + FOX_HEADS) | (r == hg + 2 * FOX_HEADS)
            qt = jnp.concatenate([qt, jnp.where(hit, 1.0, 0.0).astype(BF16)], axis=0)
        qts.append(qt)

    def k_chunk(start, n):
        k = k_ref[0, pl.ds(start, n), :]
        if extra:
            k = jnp.concatenate([k, kx_ref[0, pl.ds(start, n), :]], axis=1)
        return k

    def v_chunk(hh, start, n):
        return vt_ref[0, hh * HEAD_DIM:(hh + 1) * HEAD_DIM, pl.ds(start, n)]

    acc_ref[...] = jnp.zeros(acc_ref.shape, F32)
    rr = lax.broadcasted_iota(jnp.int32, (dch, dch), 0)
    cc = lax.broadcasted_iota(jnp.int32, (dch, dch), 1)

    def diag_ranges(c):
        off = c * dch
        out = [(off, off + dch, True)]
        if off + dch < tq:
            out.append((off + dch, tq, False))
        return out

    if mode == "softmax":
        nfull = q0 // ch
        per = ch // dch
        nb = tq // ch

        def with_ones(v):
            return jnp.concatenate([v, jnp.ones((SUM_ROWS, v.shape[1]), BF16)], axis=0)

        def produce(hh, slot, k):
            s = _dot(k, qts[hh])
            sbuf_ref[slot, hh] = s
            stat_ref[4 * hh + 2 + slot:4 * hh + 3 + slot, :] = jnp.max(s, axis=0, keepdims=True)

        for g in range(nb):
            kb = k_chunk(pl.multiple_of(q0 + g * ch, ch), ch)
            for hh in range(2):
                sbuf_ref[2 + g, hh, :, g * ch:] = _dot(kb, qts[hh][:, g * ch:])
        kl = k_chunk(pl.multiple_of(jnp.maximum(nfull - 1, 0) * ch, ch), ch)
        for hh in range(2):
            produce(hh, 1, kl)

        def band_scores(hh, c, c0, c1, tri):
            g, r = divmod(c, per)
            s = sbuf_ref[2 + g, hh, r * dch:(r + 1) * dch, c0:c1]
            return jnp.where(rr <= cc, s, MASK_VALUE) if tri else s

        for hh in range(2):
            bmax = [None] * nd
            for c in range(nd):
                for c0, c1, tri in diag_ranges(c):
                    cm = jnp.max(band_scores(hh, c, c0, c1, tri), axis=0, keepdims=True)
                    for blk in range(c0 // dch, c1 // dch):
                        part = cm[:, blk * dch - c0:(blk + 1) * dch - c0]
                        bmax[blk] = part if bmax[blk] is None else jnp.maximum(bmax[blk], part)
            m_band = jnp.concatenate(bmax, axis=1)
            stat_ref[4 * hh:4 * hh + 1, :] = m_band
            for c in range(nd):
                off = c * dch
                ps = [jnp.exp2(band_scores(hh, c, c0, c1, tri) - m_band[:, c0:c1])
                      for c0, c1, tri in diag_ranges(c)]
                p = ps[0] if len(ps) == 1 else jnp.concatenate(ps, axis=1)
                pv = _dot(with_ones(v_chunk(hh, pl.multiple_of(q0 + off, dch), dch)), p.astype(BF16))
                if c == 0:
                    acc_ref[hh] = pv
                else:
                    acc_ref[hh, :, off:] = acc_ref[hh, :, off:] + pv

        jlo = 0
        if extra == "bias":
            tab_k = tab_ref[0, 0]
            tab_b = tab_ref[0, 1]
            tiles = tab_k.shape[0]
            lane = lax.broadcasted_iota(jnp.int32, tab_k.shape, 1)
            dead = None
            for hh in range(2):
                pick = lane == 2 * pi + hh
                k_hi = jnp.sqrt(jnp.sum(jnp.where(pick, tab_k, 0.0), axis=1, keepdims=True) * 1.02)
                b_top = jnp.sum(jnp.where(pick, tab_b, 0.0), axis=1, keepdims=True)
                b_hi = b_top + jnp.abs(b_top) * (2.0 ** -7)
                qf = qts[hh][0:LANES, :].astype(F32)
                q_hi = jnp.sqrt(jnp.max(jnp.sum(qf * qf, axis=0, keepdims=True), axis=1, keepdims=True) * 1.0001)
                m_lo = jnp.min(stat_ref[4 * hh:4 * hh + 1, :], axis=1, keepdims=True)
                gone = k_hi * q_hi + b_hi - m_lo <= SCORE_DEAD
                dead = gone if dead is None else dead & gone
            tile_idx = lax.broadcasted_iota(jnp.int32, (tiles, 1), 0)
            first_live = jnp.min(jnp.where(dead, tiles, tile_idx))
            jlo = jnp.minimum(first_live * (key_tile // ch), nfull)
            jlo = jlo - lax.rem(jlo, 2)

        def half_step(j, slot, prefetch):
            start = pl.multiple_of(j * ch, ch)
            if prefetch:
                kn = k_chunk(pl.multiple_of(start - ch, ch), ch)
            for hh in range(2):
                if prefetch:
                    produce(hh, 1 - slot, kn)
                m_prev = stat_ref[4 * hh:4 * hh + 1, :]
                m_new = jnp.maximum(m_prev, stat_ref[4 * hh + 2 + slot:4 * hh + 3 + slot, :])
                stat_ref[4 * hh:4 * hh + 1, :] = m_new
                p = jnp.exp2(sbuf_ref[slot, hh] - m_new)
                pv = _dot(with_ones(v_chunk(hh, start, ch)), p.astype(BF16))
                acc_ref[hh] = jnp.exp2(m_prev - m_new) * acc_ref[hh] + pv

        def body(t, carry):
            j = nfull - 1 - 2 * t
            half_step(j, 1, True)
            half_step(j - 1, 0, True)
            return carry

        npair = (nfull - jlo) // 2
        lax.fori_loop(0, npair - 1, body, 0)

        @pl.when(npair > 0)
        def _():
            half_step(jlo + 1, 1, True)
            half_step(jlo, 0, False)

        o_t = jnp.concatenate([acc_ref[hh, 0:HEAD_DIM, :] / acc_ref[hh, HEAD_DIM:HEAD_DIM + 1, :]
                               for hh in range(2)], axis=0)
    else:
        hl_ref = refs[4]
        nfull = q0 // dch
        stat_ref[...] = jnp.zeros(stat_ref.shape, F32)

        def mask_first(x, fill):
            first = jnp.where(rr < cc, x[:, 0:dch], fill)
            return first if x.shape[1] == dch else jnp.concatenate([first, x[:, dch:]], axis=1)

        def produce(hh, slot, k, lo, hi, diag):
            z = _dot(k, qts[hh][:, lo:hi])
            sp = jnp.maximum(z, 0.0) + jnp.log(1.0 + jnp.exp2(jnp.abs(z) * -LOG2E))
            if diag:
                sp = mask_first(sp, 0.0)
            sbuf_ref[slot, hh, :, lo:hi] = z - sp
            hl_ref[slot, hh, :, lo:hi] = sp.astype(BF16)
            stat_ref[2 + 2 * slot + hh:3 + 2 * slot + hh, lo:hi] = sp[0:1, :]

        def suffix_sums(hh, slot, lo, hi):
            u = u_ref[...]
            return _dot(u, hl_ref[slot, hh, :, lo:hi])

        def consume(hh, slot, remain, start, lo, hi, diag):
            carry = stat_ref[hh:hh + 1, lo:hi]
            first = stat_ref[2 + 2 * slot + hh:3 + 2 * slot + hh, lo:hi]
            stat_ref[hh:hh + 1, lo:hi] = carry + remain[0:1, :] - first
            w = jnp.exp(sbuf_ref[slot, hh, :, lo:hi] + remain + carry)
            if diag:
                w = mask_first(w, 0.0)
            acc_ref[hh, :, lo:hi] = acc_ref[hh, :, lo:hi] + _dot(v_chunk(hh, start, dch), w.astype(BF16))

        for c in reversed(range(nd)):
            kc = k_chunk(pl.multiple_of(q0 + c * dch, dch), dch)
            for hh in range(2):
                produce(hh, 2 + c, kc, c * dch, tq, True)
        ahead = [suffix_sums(hh, 1 + nd, (nd - 1) * dch, tq) for hh in range(2)]
        for c in reversed(range(nd)):
            remain = ahead
            if c > 0:
                ahead = [suffix_sums(hh, 1 + c, (c - 1) * dch, tq) for hh in range(2)]
            for hh in range(2):
                consume(hh, 2 + c, remain[hh], pl.multiple_of(q0 + c * dch, dch), c * dch, tq, True)

        def stick_left(lo, hi):
            return (jnp.max(stat_ref[0:2, lo:hi]) > STICK_DEAD).astype(jnp.int32)

        left_first = stick_left(0, dch)
        left_rest = stick_left(dch, tq)

        def half_step(j, slot, prefetch):
            start = pl.multiple_of(j * dch, dch)
            if prefetch:
                kn = k_chunk(pl.multiple_of((j - 1) * dch, dch), dch)
            for hh in range(2):
                remain = suffix_sums(hh, slot, 0, tq)
                if prefetch:
                    produce(hh, 1 - slot, kn, 0, tq, False)
                consume(hh, slot, remain, start, 0, tq, False)

        @pl.when((nfull > 0) & (left_rest > 0))
        def _():
            kl = k_chunk(pl.multiple_of((nfull - 1) * dch, dch), dch)
            for hh in range(2):
                produce(hh, 1, kl, 0, tq, False)

            def pair(state):
                t, _ = state
                j = nfull - 1 - 2 * t
                half_step(j, 1, True)
                half_step(j - 1, 0, True)
                return t + 1, stick_left(0, tq)

            npair = nfull // 2
            _, alive = lax.while_loop(lambda st: (st[0] < npair - 1) & (st[1] > 0), pair,
                                      (jnp.int32(0), jnp.int32(1)))

            @pl.when(alive > 0)
            def _():
                half_step(1, 1, True)
                half_step(0, 0, False)

        @pl.when((nfull > 0) & (left_rest == 0) & (left_first > 0))
        def _():
            def single(state):
                j, _ = state
                start = pl.multiple_of(j * dch, dch)
                kc = k_chunk(start, dch)
                for hh in range(2):
                    produce(hh, 0, kc, 0, dch, False)
                    consume(hh, 0, suffix_sums(hh, 0, 0, dch), start, 0, dch, False)
                return j - 1, stick_left(0, dch)

            lax.while_loop(lambda st: (st[0] >= 0) & (st[1] > 0), single, (nfull - 1, jnp.int32(1)))

        o_t = jnp.concatenate([acc_ref[0], acc_ref[1]], axis=0)

    o_ref[0] = o_t.T.astype(o_ref.dtype)


def _attention(qt, k, vt, kx, *, mode, extra, tq, ch, dch, bounds=None):
    b, s, width = k.shape
    heads = qt.shape[1]
    kd = qt.shape[2]
    in_specs = [pl.BlockSpec((1, 2, kd, tq), lambda bi, pi, qi: (bi, pi, 0, qi)),
                pl.BlockSpec((1, s, LANES), lambda bi, pi, qi: (bi, 0, pi))]
    args = [qt, k]
    key_tile = ch
    if extra:
        in_specs.append(pl.BlockSpec((1, s, LANES), lambda bi, pi, qi: (bi, 0, 0)))
        args.append(kx)
    if extra == "bias":
        in_specs.append(pl.BlockSpec((1,) + bounds.shape[1:], lambda bi, pi, qi: (bi, 0, 0, 0)))
        args.append(bounds)
        key_tile = s // bounds.shape[2]
        assert key_tile % ch == 0
    in_specs.append(pl.BlockSpec((1, LANES, s), lambda bi, pi, qi: (bi, pi, 0)))
    args.append(vt)
    acc_rows = HEAD_DIM + (SUM_ROWS if mode == "softmax" else 0)
    nslot = 2 + tq // dch
    stat_rows = 8 * pl.cdiv(2 + 2 * nslot, 8)
    scratch = [pltpu.VMEM((2, acc_rows, tq), F32), pltpu.VMEM((stat_rows, tq), F32)]
    if mode == "softmax":
        scratch.append(pltpu.VMEM((2 + tq // ch, 2, ch, tq), F32))
    else:
        upper = jnp.where(lax.broadcasted_iota(jnp.int32, (dch, dch), 1)
                          > lax.broadcasted_iota(jnp.int32, (dch, dch), 0), -1.0, 0.0).astype(BF16)
        in_specs.append(pl.BlockSpec((dch, dch), lambda bi, pi, qi: (0, 0)))
        args.append(upper)
        scratch += [pltpu.VMEM((nslot, 2, dch, tq), F32),
                    pltpu.VMEM((nslot, 2, dch, tq), BF16)]
    kernel = functools.partial(_attn_kernel, mode=mode, extra=extra, tq=tq, ch=ch, dch=dch, key_tile=key_tile)
    return pl.pallas_call(
        kernel,
        out_shape=jax.ShapeDtypeStruct((b, s, width), BF16),
        grid=(b, heads // 2, s // tq),
        in_specs=in_specs,
        out_specs=pl.BlockSpec((1, tq, LANES), lambda bi, pi, qi: (bi, qi, pi)),
        scratch_shapes=scratch,
        compiler_params=pltpu.CompilerParams(
            dimension_semantics=("arbitrary", "arbitrary", "arbitrary"), vmem_limit_bytes=VMEM_LIMIT),
        name="attn_" + mode + ("_" + extra if extra else ""),
    )(*args)


def _out_kernel(*refs, n_o):
    o_refs = refs[:n_o]
    gate_ref, x_ref, w_ref, g_ref, out_ref = refs[n_o:]
    gate = _silu(gate_ref[...].astype(F32))
    y = None
    col = 0
    for o_ref in o_refs:
        wdt = o_ref.shape[1]
        og = (o_ref[...].astype(F32) * gate[:, col:col + wdt]).astype(BF16)
        part = _dot(og, w_ref[col:col + wdt, :])
        y = part if y is None else y + part
        col += wdt
    out_ref[...] = x_ref[...] + _rms(y, g_ref[...])


def _out_stage(os, gate, x, w, g, tm):
    n, d = x.shape
    os = [o.reshape(n, o.shape[-1]) for o in os]
    in_specs = [pl.BlockSpec((tm, o.shape[1]), lambda i: (i, 0)) for o in os]
    in_specs += [pl.BlockSpec((tm, gate.shape[1]), lambda i: (i, 0)),
                 pl.BlockSpec((tm, d), lambda i: (i, 0)),
                 pl.BlockSpec(w.shape, lambda i: (0, 0)),
                 pl.BlockSpec((1, d), lambda i: (0, 0))]
    return pl.pallas_call(
        functools.partial(_out_kernel, n_o=len(os)),
        out_shape=jax.ShapeDtypeStruct((n, d), F32),
        grid=(n // tm,),
        in_specs=in_specs,
        out_specs=pl.BlockSpec((tm, d), lambda i: (i, 0)),
        compiler_params=pltpu.CompilerParams(
            dimension_semantics=("arbitrary",), vmem_limit_bytes=VMEM_LIMIT),
        name="out_stage",
    )(*os, gate, x, w, g.reshape(1, d))


def kernel(x, positions, l0_pre_g, l0_post_g, l0_w_in, l0_q_a_g, l0_w_q_b, l0_kv_a_g, l0_w_kv_b, l0_w_out,
           l1_pre_g, l1_post_g, l1_w_in, l1_b_f, l1_w_out):
    b, s, d = x.shape
    tm = min(ROW_TILE, s)
    tq = min(Q_TILE, s)
    dch = min(DIAG_CHUNK, tq // 2)
    ch = min(SOFTMAX_CHUNK, tq // 2)
    assert tq % (2 * ch) == 0 and ch % dch == 0 and s % tq == 0
    sb_scale = HEAD_DIM ** -0.5
    mla_scale = (MLA_NOPE + MLA_ROPE) ** -0.5

    w = l0_w_in
    c0 = 4 * SB_WIDTH
    c1 = c0 + MLA_Q_LORA
    c2 = c1 + MLA_KV_LORA
    c3 = c2 + MLA_ROPE
    half = MLA_ROPE // 2
    wkr = w[:, c2:c3]
    wkr_rot = jnp.concatenate([-wkr[:, half:], wkr[:, :half]], axis=1)
    lane_pad = jnp.zeros((d, LANES - MLA_ROPE), F32)
    wn0 = jnp.concatenate([w[:, 512:1024], w[:, 1536:2048], w[:, c3:], w[:, c0:c1], w[:, c1:c2],
                           wkr, lane_pad, wkr_rot, lane_pad], axis=1).astype(BF16)
    wt0 = jnp.concatenate([w[:, 0:512] * sb_scale, w[:, 1024:1536]], axis=1).T.astype(BF16)
    wqbt = (l0_w_q_b * (mla_scale * LOG2E)).T.astype(BF16)
    wkv = l0_w_kv_b.reshape(MLA_KV_LORA, MLA_HEADS, MLA_NOPE + MLA_V)
    wk = wkv[:, :, :MLA_NOPE].reshape(MLA_KV_LORA, -1).astype(BF16)
    wvt = wkv[:, :, MLA_NOPE:].reshape(MLA_KV_LORA, -1).T.astype(BF16)
    invf = ROPE_THETA ** (-jnp.arange(0, MLA_ROPE, 2, dtype=F32) / MLA_ROPE)

    qt_sb, k_sb, vt_sb, gate0, qt_mla, k_mla, kx_mla, vt_mla = _l0_in(
        x, positions, l0_pre_g, wn0, wt0, l0_q_a_g, wqbt, l0_kv_a_g, wk, wvt, invf, tm)
    o_sb = _attention(qt_sb, k_sb, vt_sb, None, mode="stick", extra=None, tq=tq, ch=ch, dch=dch)
    o_mla = _attention(qt_mla, k_mla, vt_mla, kx_mla, mode="softmax", extra="rows", tq=tq, ch=ch, dch=dch)
    x1 = _out_stage([o_sb, o_mla], gate0.reshape(b * s, -1), x.reshape(b * s, d),
                    l0_w_out.astype(BF16), l0_post_g, tm)

    w1 = l1_w_in
    f_pad = jnp.zeros((d, LANES - FOX_HEADS), F32)
    wn1 = jnp.concatenate([w1[:, 1024:2048], w1[:, 3072:4096], w1[:, 4096:], f_pad], axis=1).astype(BF16)
    wt1 = jnp.concatenate([w1[:, 0:1024] * (sb_scale * LOG2E), w1[:, 2048:3072]], axis=1).T.astype(BF16)
    qt1, k1, vt1, gate1, kx1, tile_bounds = _l1_in(x1.reshape(b, s, d), l1_pre_g, wn1, wt1, l1_b_f, tm)
    bounds = tile_bounds[:, :, 0:2, :].transpose(0, 2, 1, 3)
    o1 = _attention(qt1, k1, vt1, kx1, mode="softmax", extra="bias", tq=tq, ch=ch, dch=dch, bounds=bounds)
    out = _out_stage([o1], gate1.reshape(b * s, -1), x1, l1_w_out.astype(BF16), l1_post_g, tm)
    return out.reshape(b, s, d)
```

```python
import functools
import math

import jax
import jax.numpy as jnp
from jax import lax
from jax.experimental import pallas as pl
from jax.experimental.pallas import tpu as pltpu

D_MODEL = 1024
RMS_EPS = 1e-6
HEAD_DIM = 64
SB_HEADS = 8
SB_WIDTH = SB_HEADS * HEAD_DIM
MLA_HEADS = 8
MLA_NOPE = 64
MLA_ROPE = 32
MLA_V = 64
MLA_Q_LORA = 384
MLA_KV_LORA = 256
MLA_WIDTH = MLA_HEADS * MLA_V
ROPE_THETA = 10000.0
FOX_HEADS = 16
FOX_WIDTH = FOX_HEADS * HEAD_DIM

LANES = 128
ROW_TILE = 512
Q_TILE = 1024
SOFTMAX_CHUNK = 512
DIAG_CHUNK = 256
SUM_ROWS = 16
MASK_VALUE = -1e30
STICK_DEAD = -104.0
SCORE_DEAD = -152.0
LOG2E = math.log2(math.e)
VMEM_LIMIT = 48 * 1024 * 1024
F32 = jnp.float32
BF16 = jnp.bfloat16


def _rms(xf, g):
    var = jnp.mean(xf * xf, axis=-1, keepdims=True)
    return xf * lax.rsqrt(var + RMS_EPS) * g


def _dot(a, b):
    return jnp.dot(a, b, preferred_element_type=F32)


def _dot_nt(a, b):
    return lax.dot_general(a, b, (((1,), (1,)), ((), ())), preferred_element_type=F32)


def _silu(g):
    return g / (1.0 + jnp.exp(-g))


def _gated_out(o_refs, gate_ref, x_ref, w_ref, g_ref):
    gate = _silu(gate_ref[...].astype(F32))
    y = None
    col = 0
    for o_ref in o_refs:
        wdt = o_ref.shape[1]
        og = (o_ref[...].astype(F32) * gate[:, col:col + wdt]).astype(BF16)
        part = _dot(og, w_ref[col:col + wdt, :])
        y = part if y is None else y + part
        col += wdt
    return x_ref[...] + _rms(y, g_ref[...])


def _split3(v):
    a1 = v.astype(BF16)
    r1 = v - a1.astype(F32)
    a2 = r1.astype(BF16)
    a3 = (r1 - a2.astype(F32)).astype(BF16)
    return a1, a2, a3


def _write_padded_heads(ref, rows_t, heads, width):
    tm = rows_t.shape[1]
    zeros = jnp.zeros((HEAD_DIM, tm), BF16)
    for hh in range(heads):
        off = HEAD_DIM * (hh % 2)
        ref[0, hh, off:off + HEAD_DIM, :] = rows_t[hh * width:hh * width + HEAD_DIM].astype(BF16)
        ref[0, hh, HEAD_DIM - off:2 * HEAD_DIM - off, :] = zeros


def _l0_in_kernel(x_ref, posr_ref, posc_ref, preg_ref, wn_ref, wt_ref, qag_ref, wqbt_ref,
                  kvag_ref, wk_ref, wvt_ref, invfr_ref, invfc_ref,
                  qtsb_ref, ksb_ref, vtsb_ref, gate_ref, qtmla_ref, kmla_ref, kxmla_ref, vtmla_ref):
    x = x_ref[0]
    h = _rms(x, preg_ref[...]).astype(BF16)

    def nat(a, b):
        return _dot(h, wn_ref[:, a:b])

    ksb_ref[0] = nat(0, 512).astype(BF16)
    gate_ref[0] = nat(512, 1536).astype(BF16)
    qa = nat(1536, 1920)
    ckv = nat(1920, 2176)
    kr = nat(2176, 2304)
    krr = nat(2304, 2432)

    tr = _dot_nt(wt_ref[...], h)
    _write_padded_heads(qtsb_ref, tr[0:SB_WIDTH], SB_HEADS, HEAD_DIM)
    vtsb_ref[0] = tr[SB_WIDTH:2 * SB_WIDTH].astype(BF16)

    qan = _rms(qa, qag_ref[...]).astype(BF16)
    qt = _dot_nt(wqbt_ref[...], qan)
    angt = invfc_ref[...] * posr_ref[0].astype(F32)
    cost = jnp.cos(angt)
    sint = jnp.sin(angt)
    tm = qt.shape[1]
    qtmla_ref[...] = jnp.zeros(qtmla_ref.shape, BF16)
    per_head = MLA_NOPE + MLA_ROPE
    half = MLA_ROPE // 2
    for hh in range(MLA_HEADS):
        base = hh * per_head
        off = MLA_NOPE * (hh % 2)
        qtmla_ref[0, hh, off:off + MLA_NOPE, :] = qt[base:base + MLA_NOPE].astype(BF16)
        x1 = qt[base + MLA_NOPE:base + MLA_NOPE + half]
        x2 = qt[base + MLA_NOPE + half:base + per_head]
        qtmla_ref[0, hh, LANES:LANES + half, :] = (x1 * cost - x2 * sint).astype(BF16)
        qtmla_ref[0, hh, LANES + half:LANES + MLA_ROPE, :] = (x2 * cost + x1 * sint).astype(BF16)

    ckvn = _rms(ckv, kvag_ref[...]).astype(BF16)
    kmla_ref[0] = _dot(ckvn, wk_ref[...]).astype(BF16)
    vtmla_ref[0] = _dot_nt(wvt_ref[...], ckvn).astype(BF16)
    ang = posc_ref[0].astype(F32) * invfr_ref[...]
    kxmla_ref[0] = (kr * jnp.cos(ang) + krr * jnp.sin(ang)).astype(BF16)


def _l0_in(x, positions, pre_g, wn, wt, qag, wqbt, kvag, wk, wvt, invf, tm):
    b, s, d = x.shape
    nt = s // tm
    posr = positions.reshape(b, 1, s)
    posc = positions.reshape(b, s, 1)
    invfr = jnp.tile(invf, LANES // invf.shape[0]).reshape(1, LANES)
    invfc = invf.reshape(-1, 1)
    const = lambda shape: pl.BlockSpec(shape, lambda bi, ti: (0,) * len(shape))
    row = lambda w: pl.BlockSpec((1, tm, w), lambda bi, ti: (bi, ti, 0))
    colt = lambda r: pl.BlockSpec((1, r, tm), lambda bi, ti: (bi, 0, ti))
    headt = lambda hn, r: pl.BlockSpec((1, hn, r, tm), lambda bi, ti: (bi, 0, 0, ti))
    out_shape = (
        jax.ShapeDtypeStruct((b, SB_HEADS, LANES, s), BF16),
        jax.ShapeDtypeStruct((b, s, SB_WIDTH), BF16),
        jax.ShapeDtypeStruct((b, SB_WIDTH, s), BF16),
        jax.ShapeDtypeStruct((b, s, SB_WIDTH + MLA_WIDTH), BF16),
        jax.ShapeDtypeStruct((b, MLA_HEADS, 2 * LANES, s), BF16),
        jax.ShapeDtypeStruct((b, s, MLA_WIDTH), BF16),
        jax.ShapeDtypeStruct((b, s, LANES), BF16),
        jax.ShapeDtypeStruct((b, MLA_WIDTH, s), BF16),
    )
    return pl.pallas_call(
        _l0_in_kernel,
        out_shape=out_shape,
        grid=(b, nt),
        in_specs=[
            row(d),
            pl.BlockSpec((1, 1, tm), lambda bi, ti: (bi, 0, ti)),
            pl.BlockSpec((1, tm, 1), lambda bi, ti: (bi, ti, 0)),
            const((1, d)), const(wn.shape), const(wt.shape), const((1, MLA_Q_LORA)), const(wqbt.shape),
            const((1, MLA_KV_LORA)), const(wk.shape), const(wvt.shape),
            const((1, LANES)), const((MLA_ROPE // 2, 1)),
        ],
        out_specs=(headt(SB_HEADS, LANES), row(SB_WIDTH), colt(SB_WIDTH), row(SB_WIDTH + MLA_WIDTH),
                   headt(MLA_HEADS, 2 * LANES), row(MLA_WIDTH), row(LANES), colt(MLA_WIDTH)),
        compiler_params=pltpu.CompilerParams(
            dimension_semantics=("arbitrary", "arbitrary"), vmem_limit_bytes=VMEM_LIMIT),
        name="l0_in",
    )(x, posr, posc, pre_g.reshape(1, d), wn, wt, qag.reshape(1, -1), wqbt,
      kvag.reshape(1, -1), wk, wvt, invfr, invfc)


def _l1_in_kernel(oa_ref, ob_ref, gate0_ref, x_ref, wout_ref, postg_ref,
                  preg_ref, wn_ref, wt_ref, bf_ref, tri_ref, place_ref, group_ref,
                  x1_ref, qt_ref, k_ref, vt_ref, gate_ref, kx_ref, bounds_ref, carry_ref):
    @pl.when(pl.program_id(1) == 0)
    def _():
        carry_ref[...] = jnp.zeros(carry_ref.shape, F32)

    x = _gated_out([oa_ref.at[0], ob_ref.at[0]], gate0_ref.at[0], x_ref.at[0], wout_ref, postg_ref)
    x1_ref[0] = x
    h = _rms(x, preg_ref[...]).astype(BF16)
    kb = _dot(h, wn_ref[:, 0:1024]).astype(BF16)
    k_ref[0] = kb
    gate_ref[0] = _dot(h, wn_ref[:, 1024:2048]).astype(BF16)
    f = _dot(h, wn_ref[:, 2048:2176])[:, 0:FOX_HEADS] + bf_ref[...]

    tr = _dot_nt(wt_ref[...], h)
    _write_padded_heads(qt_ref, tr[0:FOX_WIDTH], FOX_HEADS, HEAD_DIM)
    vt_ref[0] = tr[FOX_WIDTH:2 * FOX_WIDTH].astype(BF16)

    logf = jnp.minimum(f, 0.0) - jnp.log1p(jnp.exp(-jnp.abs(f)))
    tri = tri_ref[...]
    c = carry_ref[0:1, 0:FOX_HEADS]
    for term in _split3(logf):
        c = c + _dot(tri, term)
    tm = c.shape[0]
    carry_ref[0:1, 0:FOX_HEADS] = c[tm - 1:tm, :]
    slab = jnp.zeros((tm, LANES), F32)
    for t, term in enumerate(_split3(-LOG2E * c)):
        slab = slab + _dot(term, place_ref[t])
    kx_ref[0] = slab.astype(BF16)
    kf = kb.astype(F32)
    top = jnp.broadcast_to(jnp.max(kf * kf, axis=0, keepdims=True), (8, FOX_WIDTH)).astype(BF16)
    ksq = _dot(top, group_ref[...])
    bounds_ref[0, 0] = jnp.concatenate(
        [ksq[0:1, :], jnp.max(slab, axis=0, keepdims=True), jnp.zeros((6, LANES), F32)], axis=0)


def _l1_in(o_a, o_b, gate0, x, w_out0, post_g0, pre_g, wn, wt, bf, tm):
    b, s, d = x.shape
    nt = s // tm
    tri = (lax.broadcasted_iota(jnp.int32, (tm, tm), 1)
           <= lax.broadcasted_iota(jnp.int32, (tm, tm), 0)).astype(BF16)
    hh = lax.broadcasted_iota(jnp.int32, (3, FOX_HEADS, LANES), 1)
    tt = lax.broadcasted_iota(jnp.int32, (3, FOX_HEADS, LANES), 0)
    ll = lax.broadcasted_iota(jnp.int32, (3, FOX_HEADS, LANES), 2)
    place = (ll == tt * FOX_HEADS + hh).astype(BF16)
    group = (lax.broadcasted_iota(jnp.int32, (FOX_WIDTH, LANES), 0) // HEAD_DIM
             == lax.broadcasted_iota(jnp.int32, (FOX_WIDTH, LANES), 1)).astype(BF16)
    const = lambda shape: pl.BlockSpec(shape, lambda bi, ti: (0,) * len(shape))
    row = lambda w: pl.BlockSpec((1, tm, w), lambda bi, ti: (bi, ti, 0))
    out_shape = (
        jax.ShapeDtypeStruct((b, s, d), F32),
        jax.ShapeDtypeStruct((b, FOX_HEADS, LANES, s), BF16),
        jax.ShapeDtypeStruct((b, s, FOX_WIDTH), BF16),
        jax.ShapeDtypeStruct((b, FOX_WIDTH, s), BF16),
        jax.ShapeDtypeStruct((b, s, FOX_WIDTH), BF16),
        jax.ShapeDtypeStruct((b, s, LANES), BF16),
        jax.ShapeDtypeStruct((b, nt, 8, LANES), F32),
    )
    return pl.pallas_call(
        _l1_in_kernel,
        out_shape=out_shape,
        grid=(b, nt),
        in_specs=[row(o_a.shape[2]), row(o_b.shape[2]), row(gate0.shape[2]), row(d),
                  const(w_out0.shape), const((1, d)),
                  const((1, d)), const(wn.shape), const(wt.shape),
                  const((1, FOX_HEADS)), const((tm, tm)), const(place.shape), const(group.shape)],
        out_specs=(row(d),
                   pl.BlockSpec((1, FOX_HEADS, LANES, tm), lambda bi, ti: (bi, 0, 0, ti)),
                   row(FOX_WIDTH),
                   pl.BlockSpec((1, FOX_WIDTH, tm), lambda bi, ti: (bi, 0, ti)),
                   row(FOX_WIDTH), row(LANES),
                   pl.BlockSpec((1, 1, 8, LANES), lambda bi, ti: (bi, ti, 0, 0))),
        scratch_shapes=[pltpu.VMEM((8, LANES), F32)],
        compiler_params=pltpu.CompilerParams(
            dimension_semantics=("arbitrary", "arbitrary"), vmem_limit_bytes=VMEM_LIMIT),
        name="l1_in",
    )(o_a, o_b, gate0, x, w_out0, post_g0.reshape(1, d),
      pre_g.reshape(1, d), wn, wt, bf.reshape(1, -1), tri, place, group)


def _attn_kernel(*refs, mode, extra, tq, ch, dch, key_tile):
    refs = list(refs)
    qt_ref = refs.pop(0)
    k_ref = refs.pop(0)
    kx_ref = refs.pop(0) if extra else None
    tab_ref = refs.pop(0) if extra == "bias" else None
    vt_ref = refs.pop(0)
    u_ref = refs.pop(0) if mode == "stick" else None
    o_ref, acc_ref, stat_ref, sbuf_ref = refs[:4]

    pi = pl.program_id(1)
    q0 = pl.program_id(2) * tq
    nd = tq // dch

    qts = []
    for hh in range(2):
        qt = qt_ref[0, hh]
        if extra == "bias":
            hg = 2 * pi + hh
            r = lax.broadcasted_iota(jnp.int32, (LANES, tq), 0)
            hit = (r == hg) | (r == hg + FOX_HEADS) | (r == hg + 2 * FOX_HEADS)
            qt = jnp.concatenate([qt, jnp.where(hit, 1.0, 0.0).astype(BF16)], axis=0)
        qts.append(qt)

    def k_chunk(start, n):
        k = k_ref[0, pl.ds(start, n), :]
        if extra:
            k = jnp.concatenate([k, kx_ref[0, pl.ds(start, n), :]], axis=1)
        return k

    def v_chunk(hh, start, n):
        return vt_ref[0, hh * HEAD_DIM:(hh + 1) * HEAD_DIM, pl.ds(start, n)]

    acc_ref[...] = jnp.zeros(acc_ref.shape, F32)
    rr = lax.broadcasted_iota(jnp.int32, (dch, dch), 0)
    cc = lax.broadcasted_iota(jnp.int32, (dch, dch), 1)

    def diag_ranges(c):
        off = c * dch
        out = [(off, off + dch, True)]
        if off + dch < tq:
            out.append((off + dch, tq, False))
        return out

    if mode == "softmax":
        nfull = q0 // ch
        per = ch // dch
        nb = tq // ch

        def with_ones(v):
            return jnp.concatenate([v, jnp.ones((SUM_ROWS, v.shape[1]), BF16)], axis=0)

        def produce(hh, slot, k):
            s = _dot(k, qts[hh])
            sbuf_ref[slot, hh] = s
            stat_ref[4 * hh + 2 + slot:4 * hh + 3 + slot, :] = jnp.max(s, axis=0, keepdims=True)

        for g in range(nb):
            kb = k_chunk(pl.multiple_of(q0 + g * ch, ch), ch)
            for hh in range(2):
                sbuf_ref[2 + g, hh, :, g * ch:] = _dot(kb, qts[hh][:, g * ch:])
        kl = k_chunk(pl.multiple_of(jnp.maximum(nfull - 1, 0) * ch, ch), ch)
        for hh in range(2):
            produce(hh, 1, kl)

        def band_scores(hh, c, c0, c1, tri):
            g, r = divmod(c, per)
            s = sbuf_ref[2 + g, hh, r * dch:(r + 1) * dch, c0:c1]
            return jnp.where(rr <= cc, s, MASK_VALUE) if tri else s

        for hh in range(2):
            bmax = [None] * nd
            for c in range(nd):
                for c0, c1, tri in diag_ranges(c):
                    cm = jnp.max(band_scores(hh, c, c0, c1, tri), axis=0, keepdims=True)
                    for blk in range(c0 // dch, c1 // dch):
                        part = cm[:, blk * dch - c0:(blk + 1) * dch - c0]
                        bmax[blk] = part if bmax[blk] is None else jnp.maximum(bmax[blk], part)
            m_band = jnp.concatenate(bmax, axis=1)
            stat_ref[4 * hh:4 * hh + 1, :] = m_band
            for c in range(nd):
                off = c * dch
                ps = [jnp.exp2(band_scores(hh, c, c0, c1, tri) - m_band[:, c0:c1])
                      for c0, c1, tri in diag_ranges(c)]
                p = ps[0] if len(ps) == 1 else jnp.concatenate(ps, axis=1)
                pv = _dot(with_ones(v_chunk(hh, pl.multiple_of(q0 + off, dch), dch)), p.astype(BF16))
                if c == 0:
                    acc_ref[hh] = pv
                else:
                    acc_ref[hh, :, off:] = acc_ref[hh, :, off:] + pv

        jlo = 0
        if extra == "bias":
            tab_k = tab_ref[0, 0]
            tab_b = tab_ref[0, 1]
            tiles = tab_k.shape[0]
            lane = lax.broadcasted_iota(jnp.int32, tab_k.shape, 1)
            dead = None
            for hh in range(2):
                pick = lane == 2 * pi + hh
                k_hi = jnp.sqrt(jnp.sum(jnp.where(pick, tab_k, 0.0), axis=1, keepdims=True) * 1.02)
                b_top = jnp.sum(jnp.where(pick, tab_b, 0.0), axis=1, keepdims=True)
                b_hi = b_top + jnp.abs(b_top) * (2.0 ** -7)
                qf = qts[hh][0:LANES, :].astype(F32)
                q_hi = jnp.sqrt(jnp.max(jnp.sum(qf * qf, axis=0, keepdims=True), axis=1, keepdims=True) * 1.0001)
                m_lo = jnp.min(stat_ref[4 * hh:4 * hh + 1, :], axis=1, keepdims=True)
                gone = k_hi * q_hi + b_hi - m_lo <= SCORE_DEAD
                dead = gone if dead is None else dead & gone
            tile_idx = lax.broadcasted_iota(jnp.int32, (tiles, 1), 0)
            first_live = jnp.min(jnp.where(dead, tiles, tile_idx))
            jlo = jnp.minimum(first_live * (key_tile // ch), nfull)
            jlo = jlo - lax.rem(jlo, 2)

        def half_step(j, slot, prefetch):
            start = pl.multiple_of(j * ch, ch)
            if prefetch:
                kn = k_chunk(pl.multiple_of(start - ch, ch), ch)
            for hh in range(2):
                if prefetch:
                    produce(hh, 1 - slot, kn)
                m_prev = stat_ref[4 * hh:4 * hh + 1, :]
                m_new = jnp.maximum(m_prev, stat_ref[4 * hh + 2 + slot:4 * hh + 3 + slot, :])
                stat_ref[4 * hh:4 * hh + 1, :] = m_new
                p = jnp.exp2(sbuf_ref[slot, hh] - m_new)
                pv = _dot(with_ones(v_chunk(hh, start, ch)), p.astype(BF16))
                acc_ref[hh] = jnp.exp2(m_prev - m_new) * acc_ref[hh] + pv

        def body(t, carry):
            j = nfull - 1 - 2 * t
            half_step(j, 1, True)
            half_step(j - 1, 0, True)
            return carry

        npair = (nfull - jlo) // 2
        lax.fori_loop(0, npair - 1, body, 0)

        @pl.when(npair > 0)
        def _():
            half_step(jlo + 1, 1, True)
            half_step(jlo, 0, False)

        o_t = jnp.concatenate([acc_ref[hh, 0:HEAD_DIM, :] / acc_ref[hh, HEAD_DIM:HEAD_DIM + 1, :]
                               for hh in range(2)], axis=0)
    else:
        hl_ref = refs[4]
        nfull = q0 // dch
        stat_ref[...] = jnp.zeros(stat_ref.shape, F32)

        def mask_first(x, fill):
            first = jnp.where(rr < cc, x[:, 0:dch], fill)
            return first if x.shape[1] == dch else jnp.concatenate([first, x[:, dch:]], axis=1)

        def produce(hh, slot, k, lo, hi, diag):
            z = _dot(k, qts[hh][:, lo:hi])
            sp = jnp.maximum(z, 0.0) + jnp.log(1.0 + jnp.exp2(jnp.abs(z) * -LOG2E))
            if diag:
                sp = mask_first(sp, 0.0)
            sbuf_ref[slot, hh, :, lo:hi] = z - sp
            hl_ref[slot, hh, :, lo:hi] = sp.astype(BF16)
            stat_ref[2 + 2 * slot + hh:3 + 2 * slot + hh, lo:hi] = sp[0:1, :]

        def suffix_sums(hh, slot, lo, hi):
            u = u_ref[...]
            return _dot(u, hl_ref[slot, hh, :, lo:hi])

        def consume(hh, slot, remain, start, lo, hi, diag):
            carry = stat_ref[hh:hh + 1, lo:hi]
            first = stat_ref[2 + 2 * slot + hh:3 + 2 * slot + hh, lo:hi]
            stat_ref[hh:hh + 1, lo:hi] = carry + remain[0:1, :] - first
            w = jnp.exp(sbuf_ref[slot, hh, :, lo:hi] + remain + carry)
            if diag:
                w = mask_first(w, 0.0)
            acc_ref[hh, :, lo:hi] = acc_ref[hh, :, lo:hi] + _dot(v_chunk(hh, start, dch), w.astype(BF16))

        for c in reversed(range(nd)):
            kc = k_chunk(pl.multiple_of(q0 + c * dch, dch), dch)
            for hh in range(2):
                produce(hh, 2 + c, kc, c * dch, tq, True)
        ahead = [suffix_sums(hh, 1 + nd, (nd - 1) * dch, tq) for hh in range(2)]
        for c in reversed(range(nd)):
            remain = ahead
            if c > 0:
                ahead = [suffix_sums(hh, 1 + c, (c - 1) * dch, tq) for hh in range(2)]
            for hh in range(2):
                consume(hh, 2 + c, remain[hh], pl.multiple_of(q0 + c * dch, dch), c * dch, tq, True)

        def stick_left(lo, hi):
            return (jnp.max(stat_ref[0:2, lo:hi]) > STICK_DEAD).astype(jnp.int32)

        left_first = stick_left(0, dch)
        left_rest = stick_left(dch, tq)

        def half_step(j, slot, prefetch):
            start = pl.multiple_of(j * dch, dch)
            if prefetch:
                kn = k_chunk(pl.multiple_of((j - 1) * dch, dch), dch)
            for hh in range(2):
                remain = suffix_sums(hh, slot, 0, tq)
                if prefetch:
                    produce(hh, 1 - slot, kn, 0, tq, False)
                consume(hh, slot, remain, start, 0, tq, False)

        @pl.when((nfull > 0) & (left_rest > 0))
        def _():
            kl = k_chunk(pl.multiple_of((nfull - 1) * dch, dch), dch)
            for hh in range(2):
                produce(hh, 1, kl, 0, tq, False)

            def pair(state):
                t, _ = state
                j = nfull - 1 - 2 * t
                half_step(j, 1, True)
                half_step(j - 1, 0, True)
                return t + 1, stick_left(0, tq)

            npair = nfull // 2
            _, alive = lax.while_loop(lambda st: (st[0] < npair - 1) & (st[1] > 0), pair,
                                      (jnp.int32(0), jnp.int32(1)))

            @pl.when(alive > 0)
            def _():
                half_step(1, 1, True)
                half_step(0, 0, False)

        @pl.when((nfull > 0) & (left_rest == 0) & (left_first > 0))
        def _():
            def single(state):
                j, _ = state
                start = pl.multiple_of(j * dch, dch)
                kc = k_chunk(start, dch)
                for hh in range(2):
                    produce(hh, 0, kc, 0, dch, False)
                    consume(hh, 0, suffix_sums(hh, 0, 0, dch), start, 0, dch, False)
                return j - 1, stick_left(0, dch)

            lax.while_loop(lambda st: (st[0] >= 0) & (st[1] > 0), single, (nfull - 1, jnp.int32(1)))

        o_t = jnp.concatenate([acc_ref[0], acc_ref[1]], axis=0)

    o_ref[0] = o_t.T.astype(o_ref.dtype)


def _attention(qt, k, vt, kx, *, mode, extra, tq, ch, dch, bounds=None):
    b, s, width = k.shape
    heads = qt.shape[1]
    kd = qt.shape[2]
    in_specs = [pl.BlockSpec((1, 2, kd, tq), lambda bi, pi, qi: (bi, pi, 0, qi)),
                pl.BlockSpec((1, s, LANES), lambda bi, pi, qi: (bi, 0, pi))]
    args = [qt, k]
    key_tile = ch
    if extra:
        in_specs.append(pl.BlockSpec((1, s, LANES), lambda bi, pi, qi: (bi, 0, 0)))
        args.append(kx)
    if extra == "bias":
        in_specs.append(pl.BlockSpec((1,) + bounds.shape[1:], lambda bi, pi, qi: (bi, 0, 0, 0)))
        args.append(bounds)
        key_tile = s // bounds.shape[2]
        assert key_tile % ch == 0
    in_specs.append(pl.BlockSpec((1, LANES, s), lambda bi, pi, qi: (bi, pi, 0)))
    args.append(vt)
    acc_rows = HEAD_DIM + (SUM_ROWS if mode == "softmax" else 0)
    nslot = 2 + tq // dch
    stat_rows = 8 * pl.cdiv(2 + 2 * nslot, 8)
    scratch = [pltpu.VMEM((2, acc_rows, tq), F32), pltpu.VMEM((stat_rows, tq), F32)]
    if mode == "softmax":
        scratch.append(pltpu.VMEM((2 + tq // ch, 2, ch, tq), F32))
    else:
        upper = jnp.where(lax.broadcasted_iota(jnp.int32, (dch, dch), 1)
                          > lax.broadcasted_iota(jnp.int32, (dch, dch), 0), -1.0, 0.0).astype(BF16)
        in_specs.append(pl.BlockSpec((dch, dch), lambda bi, pi, qi: (0, 0)))
        args.append(upper)
        scratch += [pltpu.VMEM((nslot, 2, dch, tq), F32),
                    pltpu.VMEM((nslot, 2, dch, tq), BF16)]
    kernel = functools.partial(_attn_kernel, mode=mode, extra=extra, tq=tq, ch=ch, dch=dch, key_tile=key_tile)
    return pl.pallas_call(
        kernel,
        out_shape=jax.ShapeDtypeStruct((b, s, width), BF16),
        grid=(b, heads // 2, s // tq),
        in_specs=in_specs,
        out_specs=pl.BlockSpec((1, tq, LANES), lambda bi, pi, qi: (bi, qi, pi)),
        scratch_shapes=scratch,
        compiler_params=pltpu.CompilerParams(
            dimension_semantics=("arbitrary", "arbitrary", "arbitrary"), vmem_limit_bytes=VMEM_LIMIT),
        name="attn_" + mode + ("_" + extra if extra else ""),
    )(*args)


def _out_kernel(*refs, n_o):
    o_refs = refs[:n_o]
    gate_ref, x_ref, w_ref, g_ref, out_ref = refs[n_o:]
    out_ref[...] = _gated_out(o_refs, gate_ref, x_ref, w_ref, g_ref)


def _out_stage(os, gate, x, w, g, tm):
    n, d = x.shape
    os = [o.reshape(n, o.shape[-1]) for o in os]
    in_specs = [pl.BlockSpec((tm, o.shape[1]), lambda i: (i, 0)) for o in os]
    in_specs += [pl.BlockSpec((tm, gate.shape[1]), lambda i: (i, 0)),
                 pl.BlockSpec((tm, d), lambda i: (i, 0)),
                 pl.BlockSpec(w.shape, lambda i: (0, 0)),
                 pl.BlockSpec((1, d), lambda i: (0, 0))]
    return pl.pallas_call(
        functools.partial(_out_kernel, n_o=len(os)),
        out_shape=jax.ShapeDtypeStruct((n, d), F32),
        grid=(n // tm,),
        in_specs=in_specs,
        out_specs=pl.BlockSpec((tm, d), lambda i: (i, 0)),
        compiler_params=pltpu.CompilerParams(
            dimension_semantics=("arbitrary",), vmem_limit_bytes=VMEM_LIMIT),
        name="out_stage",
    )(*os, gate, x, w, g.reshape(1, d))


def kernel(x, positions, l0_pre_g, l0_post_g, l0_w_in, l0_q_a_g, l0_w_q_b, l0_kv_a_g, l0_w_kv_b, l0_w_out,
           l1_pre_g, l1_post_g, l1_w_in, l1_b_f, l1_w_out):
    b, s, d = x.shape
    tm = min(ROW_TILE, s)
    tq = min(Q_TILE, s)
    dch = min(DIAG_CHUNK, tq // 2)
    ch = min(SOFTMAX_CHUNK, tq // 2)
    assert tq % (2 * ch) == 0 and ch % dch == 0 and s % tq == 0
    sb_scale = HEAD_DIM ** -0.5
    mla_scale = (MLA_NOPE + MLA_ROPE) ** -0.5

    w = l0_w_in
    c0 = 4 * SB_WIDTH
    c1 = c0 + MLA_Q_LORA
    c2 = c1 + MLA_KV_LORA
    c3 = c2 + MLA_ROPE
    half = MLA_ROPE // 2
    wkr = w[:, c2:c3]
    wkr_rot = jnp.concatenate([-wkr[:, half:], wkr[:, :half]], axis=1)
    lane_pad = jnp.zeros((d, LANES - MLA_ROPE), F32)
    wn0 = jnp.concatenate([w[:, 512:1024], w[:, 1536:2048], w[:, c3:], w[:, c0:c1], w[:, c1:c2],
                           wkr, lane_pad, wkr_rot, lane_pad], axis=1).astype(BF16)
    wt0 = jnp.concatenate([w[:, 0:512] * sb_scale, w[:, 1024:1536]], axis=1).T.astype(BF16)
    wqbt = (l0_w_q_b * (mla_scale * LOG2E)).T.astype(BF16)
    wkv = l0_w_kv_b.reshape(MLA_KV_LORA, MLA_HEADS, MLA_NOPE + MLA_V)
    wk = wkv[:, :, :MLA_NOPE].reshape(MLA_KV_LORA, -1).astype(BF16)
    wvt = wkv[:, :, MLA_NOPE:].reshape(MLA_KV_LORA, -1).T.astype(BF16)
    invf = ROPE_THETA ** (-jnp.arange(0, MLA_ROPE, 2, dtype=F32) / MLA_ROPE)

    qt_sb, k_sb, vt_sb, gate0, qt_mla, k_mla, kx_mla, vt_mla = _l0_in(
        x, positions, l0_pre_g, wn0, wt0, l0_q_a_g, wqbt, l0_kv_a_g, wk, wvt, invf, tm)
    o_sb = _attention(qt_sb, k_sb, vt_sb, None, mode="stick", extra=None, tq=tq, ch=ch, dch=dch)
    o_mla = _attention(qt_mla, k_mla, vt_mla, kx_mla, mode="softmax", extra="rows", tq=tq, ch=ch, dch=dch)

    w1 = l1_w_in
    f_pad = jnp.zeros((d, LANES - FOX_HEADS), F32)
    wn1 = jnp.concatenate([w1[:, 1024:2048], w1[:, 3072:4096], w1[:, 4096:], f_pad], axis=1).astype(BF16)
    wt1 = jnp.concatenate([w1[:, 0:1024] * (sb_scale * LOG2E), w1[:, 2048:3072]], axis=1).T.astype(BF16)
    x1, qt1, k1, vt1, gate1, kx1, tile_bounds = _l1_in(
        o_sb, o_mla, gate0, x, l0_w_out.astype(BF16), l0_post_g, l1_pre_g, wn1, wt1, l1_b_f, tm)
    bounds = tile_bounds[:, :, 0:2, :].transpose(0, 2, 1, 3)
    o1 = _attention(qt1, k1, vt1, kx1, mode="softmax", extra="bias", tq=tq, ch=ch, dch=dch, bounds=bounds)
    out = _out_stage([o1], gate1.reshape(b * s, -1), x1.reshape(b * s, d), l1_w_out.astype(BF16),
                     l1_post_g, tm)
    return out.reshape(b, s, d)
```

```python
import functools
import math

import jax
import jax.numpy as jnp
from jax import lax
from jax.experimental import pallas as pl
from jax.experimental.pallas import tpu as pltpu

D_MODEL = 1024
RMS_EPS = 1e-6
HEAD_DIM = 64
SB_HEADS = 8
SB_WIDTH = SB_HEADS * HEAD_DIM
MLA_HEADS = 8
MLA_NOPE = 64
MLA_ROPE = 32
MLA_V = 64
MLA_Q_LORA = 384
MLA_KV_LORA = 256
MLA_WIDTH = MLA_HEADS * MLA_V
ROPE_THETA = 10000.0
FOX_HEADS = 16
FOX_WIDTH = FOX_HEADS * HEAD_DIM

LANES = 128
ROW_TILE = 512
Q_TILE = 1024
SOFTMAX_CHUNK = 512
DIAG_CHUNK = 256
SUM_ROWS = 16
MASK_VALUE = -1e30
STICK_DEAD = -104.0
SCORE_DEAD = -152.0
LOG2E = math.log2(math.e)
VMEM_LIMIT = 48 * 1024 * 1024
F32 = jnp.float32
BF16 = jnp.bfloat16


def _rms(xf, g):
    var = jnp.mean(xf * xf, axis=-1, keepdims=True)
    return xf * lax.rsqrt(var + RMS_EPS) * g


def _dot(a, b):
    return jnp.dot(a, b, preferred_element_type=F32)


def _dot_nt(a, b):
    return lax.dot_general(a, b, (((1,), (1,)), ((), ())), preferred_element_type=F32)


def _silu(g):
    return g / (1.0 + jnp.exp(-g))


def _gated_out(o_refs, gate_ref, x_ref, w_ref, g_ref):
    gate = _silu(gate_ref[...].astype(F32))
    y = None
    col = 0
    for o_ref in o_refs:
        wdt = o_ref.shape[1]
        og = (o_ref[...].astype(F32) * gate[:, col:col + wdt]).astype(BF16)
        part = _dot(og, w_ref[col:col + wdt, :])
        y = part if y is None else y + part
        col += wdt
    return x_ref[...] + _rms(y, g_ref[...])


def _split3(v):
    a1 = v.astype(BF16)
    r1 = v - a1.astype(F32)
    a2 = r1.astype(BF16)
    a3 = (r1 - a2.astype(F32)).astype(BF16)
    return a1, a2, a3


def _write_padded_heads(ref, rows_t, heads, width):
    tm = rows_t.shape[1]
    zeros = jnp.zeros((HEAD_DIM, tm), BF16)
    for hh in range(heads):
        off = HEAD_DIM * (hh % 2)
        ref[0, hh, off:off + HEAD_DIM, :] = rows_t[hh * width:hh * width + HEAD_DIM].astype(BF16)
        ref[0, hh, HEAD_DIM - off:2 * HEAD_DIM - off, :] = zeros


def _l0_in_kernel(x_ref, posr_ref, posc_ref, preg_ref, wn_ref, wt_ref, qag_ref, wqbt_ref,
                  kvag_ref, wk_ref, wvt_ref, invfr_ref, invfc_ref,
                  qtsb_ref, ksb_ref, vtsb_ref, gate_ref, qtmla_ref, kmla_ref, kxmla_ref, vtmla_ref):
    x = x_ref[0]
    h = _rms(x, preg_ref[...]).astype(BF16)

    def nat(a, b):
        return _dot(h, wn_ref[:, a:b])

    ksb_ref[0] = nat(0, 512).astype(BF16)
    gate_ref[0] = nat(512, 1536).astype(BF16)
    qa = nat(1536, 1920)
    ckv = nat(1920, 2176)
    kr = nat(2176, 2304)
    krr = nat(2304, 2432)

    tr = _dot_nt(wt_ref[...], h)
    _write_padded_heads(qtsb_ref, tr[0:SB_WIDTH], SB_HEADS, HEAD_DIM)
    vtsb_ref[0] = tr[SB_WIDTH:2 * SB_WIDTH].astype(BF16)

    qan = _rms(qa, qag_ref[...]).astype(BF16)
    qt = _dot_nt(wqbt_ref[...], qan)
    angt = invfc_ref[...] * posr_ref[0].astype(F32)
    cost = jnp.cos(angt)
    sint = jnp.sin(angt)
    tm = qt.shape[1]
    qtmla_ref[...] = jnp.zeros(qtmla_ref.shape, BF16)
    per_head = MLA_NOPE + MLA_ROPE
    half = MLA_ROPE // 2
    for hh in range(MLA_HEADS):
        base = hh * per_head
        off = MLA_NOPE * (hh % 2)
        qtmla_ref[0, hh, off:off + MLA_NOPE, :] = qt[base:base + MLA_NOPE].astype(BF16)
        x1 = qt[base + MLA_NOPE:base + MLA_NOPE + half]
        x2 = qt[base + MLA_NOPE + half:base + per_head]
        qtmla_ref[0, hh, LANES:LANES + half, :] = (x1 * cost - x2 * sint).astype(BF16)
        qtmla_ref[0, hh, LANES + half:LANES + MLA_ROPE, :] = (x2 * cost + x1 * sint).astype(BF16)

    ckvn = _rms(ckv, kvag_ref[...]).astype(BF16)
    kmla_ref[0] = _dot(ckvn, wk_ref[...]).astype(BF16)
    vtmla_ref[0] = _dot_nt(wvt_ref[...], ckvn).astype(BF16)
    ang = posc_ref[0].astype(F32) * invfr_ref[...]
    kxmla_ref[0] = (kr * jnp.cos(ang) + krr * jnp.sin(ang)).astype(BF16)


def _l0_in(x, positions, pre_g, wn, wt, qag, wqbt, kvag, wk, wvt, invf, tm):
    b, s, d = x.shape
    nt = s // tm
    posr = positions.reshape(b, 1, s)
    posc = positions.reshape(b, s, 1)
    invfr = jnp.tile(invf, LANES // invf.shape[0]).reshape(1, LANES)
    invfc = invf.reshape(-1, 1)
    const = lambda shape: pl.BlockSpec(shape, lambda bi, ti: (0,) * len(shape))
    row = lambda w: pl.BlockSpec((1, tm, w), lambda bi, ti: (bi, ti, 0))
    colt = lambda r: pl.BlockSpec((1, r, tm), lambda bi, ti: (bi, 0, ti))
    headt = lambda hn, r: pl.BlockSpec((1, hn, r, tm), lambda bi, ti: (bi, 0, 0, ti))
    out_shape = (
        jax.ShapeDtypeStruct((b, SB_HEADS, LANES, s), BF16),
        jax.ShapeDtypeStruct((b, s, SB_WIDTH), BF16),
        jax.ShapeDtypeStruct((b, SB_WIDTH, s), BF16),
        jax.ShapeDtypeStruct((b, s, SB_WIDTH + MLA_WIDTH), BF16),
        jax.ShapeDtypeStruct((b, MLA_HEADS, 2 * LANES, s), BF16),
        jax.ShapeDtypeStruct((b, s, MLA_WIDTH), BF16),
        jax.ShapeDtypeStruct((b, s, LANES), BF16),
        jax.ShapeDtypeStruct((b, MLA_WIDTH, s), BF16),
    )
    return pl.pallas_call(
        _l0_in_kernel,
        out_shape=out_shape,
        grid=(b, nt),
        in_specs=[
            row(d),
            pl.BlockSpec((1, 1, tm), lambda bi, ti: (bi, 0, ti)),
            pl.BlockSpec((1, tm, 1), lambda bi, ti: (bi, ti, 0)),
            const((1, d)), const(wn.shape), const(wt.shape), const((1, MLA_Q_LORA)), const(wqbt.shape),
            const((1, MLA_KV_LORA)), const(wk.shape), const(wvt.shape),
            const((1, LANES)), const((MLA_ROPE // 2, 1)),
        ],
        out_specs=(headt(SB_HEADS, LANES), row(SB_WIDTH), colt(SB_WIDTH), row(SB_WIDTH + MLA_WIDTH),
                   headt(MLA_HEADS, 2 * LANES), row(MLA_WIDTH), row(LANES), colt(MLA_WIDTH)),
        compiler_params=pltpu.CompilerParams(
            dimension_semantics=("arbitrary", "arbitrary"), vmem_limit_bytes=VMEM_LIMIT),
        name="l0_in",
    )(x, posr, posc, pre_g.reshape(1, d), wn, wt, qag.reshape(1, -1), wqbt,
      kvag.reshape(1, -1), wk, wvt, invfr, invfc)


def _l1_in_kernel(oa_ref, ob_ref, gate0_ref, x_ref, wout_ref, postg_ref,
                  preg_ref, wn_ref, wt_ref, bf_ref, tri_ref, place_ref, group_ref,
                  x1_ref, qt_ref, k_ref, vt_ref, gate_ref, kx_ref, bounds_ref, carry_ref):
    @pl.when(pl.program_id(1) == 0)
    def _():
        carry_ref[...] = jnp.zeros(carry_ref.shape, F32)

    x = _gated_out([oa_ref.at[0], ob_ref.at[0]], gate0_ref.at[0], x_ref.at[0], wout_ref, postg_ref)
    x1_ref[0] = x
    h = _rms(x, preg_ref[...]).astype(BF16)
    kb = _dot(h, wn_ref[:, 0:1024]).astype(BF16)
    k_ref[0] = kb
    gate_ref[0] = _dot(h, wn_ref[:, 1024:2048]).astype(BF16)
    f = _dot(h, wn_ref[:, 2048:2176])[:, 0:FOX_HEADS] + bf_ref[...]

    tr = _dot_nt(wt_ref[...], h)
    _write_padded_heads(qt_ref, tr[0:FOX_WIDTH], FOX_HEADS, HEAD_DIM)
    vt_ref[0] = tr[FOX_WIDTH:2 * FOX_WIDTH].astype(BF16)

    logf = jnp.minimum(f, 0.0) - jnp.log1p(jnp.exp(-jnp.abs(f)))
    tri = tri_ref[...]
    c = carry_ref[0:1, 0:FOX_HEADS]
    for term in _split3(logf):
        c = c + _dot(tri, term)
    tm = c.shape[0]
    carry_ref[0:1, 0:FOX_HEADS] = c[tm - 1:tm, :]
    slab = jnp.zeros((tm, LANES), F32)
    for t, term in enumerate(_split3(-LOG2E * c)):
        slab = slab + _dot(term, place_ref[t])
    kx_ref[0] = slab.astype(BF16)
    kf = kb.astype(F32)
    top = jnp.broadcast_to(jnp.max(kf * kf, axis=0, keepdims=True), (8, FOX_WIDTH)).astype(BF16)
    ksq = _dot(top, group_ref[...])
    bounds_ref[0, 0] = jnp.concatenate(
        [ksq[0:1, :], jnp.max(slab, axis=0, keepdims=True), jnp.zeros((6, LANES), F32)], axis=0)


def _l1_in(o_a, o_b, gate0, x, w_out0, post_g0, pre_g, wn, wt, bf, tm):
    b, s, d = x.shape
    nt = s // tm
    tri = (lax.broadcasted_iota(jnp.int32, (tm, tm), 1)
           <= lax.broadcasted_iota(jnp.int32, (tm, tm), 0)).astype(BF16)
    hh = lax.broadcasted_iota(jnp.int32, (3, FOX_HEADS, LANES), 1)
    tt = lax.broadcasted_iota(jnp.int32, (3, FOX_HEADS, LANES), 0)
    ll = lax.broadcasted_iota(jnp.int32, (3, FOX_HEADS, LANES), 2)
    place = (ll == tt * FOX_HEADS + hh).astype(BF16)
    group = (lax.broadcasted_iota(jnp.int32, (FOX_WIDTH, LANES), 0) // HEAD_DIM
             == lax.broadcasted_iota(jnp.int32, (FOX_WIDTH, LANES), 1)).astype(BF16)
    const = lambda shape: pl.BlockSpec(shape, lambda bi, ti: (0,) * len(shape))
    row = lambda w: pl.BlockSpec((1, tm, w), lambda bi, ti: (bi, ti, 0))
    out_shape = (
        jax.ShapeDtypeStruct((b, s, d), F32),
        jax.ShapeDtypeStruct((b, FOX_HEADS, LANES, s), BF16),
        jax.ShapeDtypeStruct((b, s, FOX_WIDTH), BF16),
        jax.ShapeDtypeStruct((b, FOX_WIDTH, s), BF16),
        jax.ShapeDtypeStruct((b, s, FOX_WIDTH), BF16),
        jax.ShapeDtypeStruct((b, s, LANES), BF16),
        jax.ShapeDtypeStruct((b, nt, 8, LANES), F32),
    )
    return pl.pallas_call(
        _l1_in_kernel,
        out_shape=out_shape,
        grid=(b, nt),
        in_specs=[row(o_a.shape[2]), row(o_b.shape[2]), row(gate0.shape[2]), row(d),
                  const(w_out0.shape), const((1, d)),
                  const((1, d)), const(wn.shape), const(wt.shape),
                  const((1, FOX_HEADS)), const((tm, tm)), const(place.shape), const(group.shape)],
        out_specs=(row(d),
                   pl.BlockSpec((1, FOX_HEADS, LANES, tm), lambda bi, ti: (bi, 0, 0, ti)),
                   row(FOX_WIDTH),
                   pl.BlockSpec((1, FOX_WIDTH, tm), lambda bi, ti: (bi, 0, ti)),
                   row(FOX_WIDTH), row(LANES),
                   pl.BlockSpec((1, 1, 8, LANES), lambda bi, ti: (bi, ti, 0, 0))),
        scratch_shapes=[pltpu.VMEM((8, LANES), F32)],
        compiler_params=pltpu.CompilerParams(
            dimension_semantics=("arbitrary", "arbitrary"), vmem_limit_bytes=VMEM_LIMIT),
        name="l1_in",
    )(o_a, o_b, gate0, x, w_out0, post_g0.reshape(1, d),
      pre_g.reshape(1, d), wn, wt, bf.reshape(1, -1), tri, place, group)


def _attn_kernel(*refs, mode, extra, tq, ch, dch, key_tile):
    refs = list(refs)
    qt_ref = refs.pop(0)
    k_ref = refs.pop(0)
    kx_ref = refs.pop(0) if extra else None
    tab_ref = refs.pop(0) if extra == "bias" else None
    vt_ref = refs.pop(0)
    u_ref = refs.pop(0) if mode == "stick" else None
    o_ref, acc_ref, stat_ref, sbuf_ref = refs[:4]

    pi = pl.program_id(1)
    q0 = pl.program_id(2) * tq
    nd = tq // dch

    qts = []
    for hh in range(2):
        qt = qt_ref[0, hh]
        if extra == "bias":
            hg = 2 * pi + hh
            r = lax.broadcasted_iota(jnp.int32, (LANES, tq), 0)
            hit = (r == hg) | (r == hg + FOX_HEADS) | (r == hg + 2 * FOX_HEADS)
            qt = jnp.concatenate([qt, jnp.where(hit, 1.0, 0.0).astype(BF16)], axis=0)
        qts.append(qt)

    def k_chunk(start, n):
        k = k_ref[0, pl.ds(start, n), :]
        if extra:
            k = jnp.concatenate([k, kx_ref[0, pl.ds(start, n), :]], axis=1)
        return k

    def v_chunk(hh, start, n):
        return vt_ref[0, hh * HEAD_DIM:(hh + 1) * HEAD_DIM, pl.ds(start, n)]

    acc_ref[...] = jnp.zeros(acc_ref.shape, F32)
    rr = lax.broadcasted_iota(jnp.int32, (dch, dch), 0)
    cc = lax.broadcasted_iota(jnp.int32, (dch, dch), 1)

    def diag_ranges(c):
        off = c * dch
        out = [(off, off + dch, True)]
        if off + dch < tq:
            out.append((off + dch, tq, False))
        return out

    if mode == "softmax":
        nfull = q0 // ch
        per = ch // dch
        nb = tq // ch

        def with_ones(v):
            return jnp.concatenate([v, jnp.ones((SUM_ROWS, v.shape[1]), BF16)], axis=0)

        def produce(hh, slot, k):
            s = _dot(k, qts[hh])
            sbuf_ref[slot, hh] = s
            stat_ref[4 * hh + 2 + slot:4 * hh + 3 + slot, :] = jnp.max(s, axis=0, keepdims=True)

        for g in range(nb):
            kb = k_chunk(pl.multiple_of(q0 + g * ch, ch), ch)
            for hh in range(2):
                sbuf_ref[2 + g, hh, :, g * ch:] = _dot(kb, qts[hh][:, g * ch:])
        kl = k_chunk(pl.multiple_of(jnp.maximum(nfull - 1, 0) * ch, ch), ch)
        for hh in range(2):
            produce(hh, 1, kl)

        def band_scores(hh, c, c0, c1, tri):
            g, r = divmod(c, per)
            s = sbuf_ref[2 + g, hh, r * dch:(r + 1) * dch, c0:c1]
            return jnp.where(rr <= cc, s, MASK_VALUE) if tri else s

        for hh in range(2):
            bmax = [None] * nd
            for c in range(nd):
                for c0, c1, tri in diag_ranges(c):
                    cm = jnp.max(band_scores(hh, c, c0, c1, tri), axis=0, keepdims=True)
                    for blk in range(c0 // dch, c1 // dch):
                        part = cm[:, blk * dch - c0:(blk + 1) * dch - c0]
                        bmax[blk] = part if bmax[blk] is None else jnp.maximum(bmax[blk], part)
            m_band = jnp.concatenate(bmax, axis=1)
            stat_ref[4 * hh:4 * hh + 1, :] = m_band
            for c in range(nd):
                off = c * dch
                ps = [jnp.exp2(band_scores(hh, c, c0, c1, tri) - m_band[:, c0:c1])
                      for c0, c1, tri in diag_ranges(c)]
                p = ps[0] if len(ps) == 1 else jnp.concatenate(ps, axis=1)
                pv = _dot(with_ones(v_chunk(hh, pl.multiple_of(q0 + off, dch), dch)), p.astype(BF16))
                if c == 0:
                    acc_ref[hh] = pv
                else:
                    acc_ref[hh, :, off:] = acc_ref[hh, :, off:] + pv

        jlo = 0
        if extra == "bias":
            tab_k = tab_ref[0, 0]
            tab_b = tab_ref[0, 1]
            tiles = tab_k.shape[0]
            lane = lax.broadcasted_iota(jnp.int32, tab_k.shape, 1)
            dead = None
            for hh in range(2):
                pick = lane == 2 * pi + hh
                k_hi = jnp.sqrt(jnp.sum(jnp.where(pick, tab_k, 0.0), axis=1, keepdims=True) * 1.02)
                b_top = jnp.sum(jnp.where(pick, tab_b, 0.0), axis=1, keepdims=True)
                b_hi = b_top + jnp.abs(b_top) * (2.0 ** -7)
                qf = qts[hh][0:LANES, :].astype(F32)
                q_hi = jnp.sqrt(jnp.max(jnp.sum(qf * qf, axis=0, keepdims=True), axis=1, keepdims=True) * 1.0001)
                m_lo = jnp.min(stat_ref[4 * hh:4 * hh + 1, :], axis=1, keepdims=True)
                gone = k_hi * q_hi + b_hi - m_lo <= SCORE_DEAD
                dead = gone if dead is None else dead & gone
            tile_idx = lax.broadcasted_iota(jnp.int32, (tiles, 1), 0)
            first_live = jnp.min(jnp.where(dead, tiles, tile_idx))
            jlo = jnp.minimum(first_live * (key_tile // ch), nfull)

        def half_step(j, slot, prefetch):
            start = pl.multiple_of(j * ch, ch)
            if prefetch:
                kn = k_chunk(pl.multiple_of(start - ch, ch), ch)
            for hh in range(2):
                if prefetch:
                    produce(hh, 1 - slot, kn)
                m_prev = stat_ref[4 * hh:4 * hh + 1, :]
                m_new = jnp.maximum(m_prev, stat_ref[4 * hh + 2 + slot:4 * hh + 3 + slot, :])
                stat_ref[4 * hh:4 * hh + 1, :] = m_new
                p = jnp.exp2(sbuf_ref[slot, hh] - m_new)
                pv = _dot(with_ones(v_chunk(hh, start, ch)), p.astype(BF16))
                acc_ref[hh] = jnp.exp2(m_prev - m_new) * acc_ref[hh] + pv

        def body(t, carry):
            j = nfull - 1 - 2 * t
            half_step(j, 1, True)
            half_step(j - 1, 0, True)
            return carry

        nleft = nfull - jlo
        lax.fori_loop(0, (nleft + 1) // 2 - 1, body, 0)

        @pl.when((nleft > 0) & (lax.rem(nleft, 2) == 0))
        def _():
            half_step(jlo + 1, 1, True)
            half_step(jlo, 0, False)

        @pl.when(lax.rem(nleft, 2) == 1)
        def _():
            half_step(jlo, 1, False)

        o_t = jnp.concatenate([acc_ref[hh, 0:HEAD_DIM, :] / acc_ref[hh, HEAD_DIM:HEAD_DIM + 1, :]
                               for hh in range(2)], axis=0)
    else:
        hl_ref = refs[4]
        nfull = q0 // dch
        stat_ref[...] = jnp.zeros(stat_ref.shape, F32)

        def mask_first(x, fill):
            first = jnp.where(rr < cc, x[:, 0:dch], fill)
            return first if x.shape[1] == dch else jnp.concatenate([first, x[:, dch:]], axis=1)

        def produce(hh, slot, k, lo, hi, diag):
            z = _dot(k, qts[hh][:, lo:hi])
            sp = jnp.maximum(z, 0.0) + jnp.log(1.0 + jnp.exp2(jnp.abs(z) * -LOG2E))
            if diag:
                sp = mask_first(sp, 0.0)
            sbuf_ref[slot, hh, :, lo:hi] = z - sp
            hl_ref[slot, hh, :, lo:hi] = sp.astype(BF16)
            stat_ref[2 + 2 * slot + hh:3 + 2 * slot + hh, lo:hi] = sp[0:1, :]

        def suffix_sums(hh, slot, lo, hi):
            u = u_ref[...]
            return _dot(u, hl_ref[slot, hh, :, lo:hi])

        def consume(hh, slot, remain, start, lo, hi, diag):
            carry = stat_ref[hh:hh + 1, lo:hi]
            first = stat_ref[2 + 2 * slot + hh:3 + 2 * slot + hh, lo:hi]
            stat_ref[hh:hh + 1, lo:hi] = carry + remain[0:1, :] - first
            w = jnp.exp(sbuf_ref[slot, hh, :, lo:hi] + remain + carry)
            if diag:
                w = mask_first(w, 0.0)
            acc_ref[hh, :, lo:hi] = acc_ref[hh, :, lo:hi] + _dot(v_chunk(hh, start, dch), w.astype(BF16))

        for c in reversed(range(nd)):
            kc = k_chunk(pl.multiple_of(q0 + c * dch, dch), dch)
            for hh in range(2):
                produce(hh, 2 + c, kc, c * dch, tq, True)
        ahead = [suffix_sums(hh, 1 + nd, (nd - 1) * dch, tq) for hh in range(2)]
        for c in reversed(range(nd)):
            remain = ahead
            if c > 0:
                ahead = [suffix_sums(hh, 1 + c, (c - 1) * dch, tq) for hh in range(2)]
            for hh in range(2):
                consume(hh, 2 + c, remain[hh], pl.multiple_of(q0 + c * dch, dch), c * dch, tq, True)

        def stick_left(lo, hi):
            return (jnp.max(stat_ref[0:2, lo:hi]) > STICK_DEAD).astype(jnp.int32)

        left_first = stick_left(0, dch)
        left_rest = stick_left(dch, tq)

        def half_step(j, slot, prefetch):
            start = pl.multiple_of(j * dch, dch)
            if prefetch:
                kn = k_chunk(pl.multiple_of((j - 1) * dch, dch), dch)
            for hh in range(2):
                remain = suffix_sums(hh, slot, 0, tq)
                if prefetch:
                    produce(hh, 1 - slot, kn, 0, tq, False)
                consume(hh, slot, remain, start, 0, tq, False)

        @pl.when((nfull > 0) & (left_rest > 0))
        def _():
            kl = k_chunk(pl.multiple_of((nfull - 1) * dch, dch), dch)
            for hh in range(2):
                produce(hh, 1, kl, 0, tq, False)

            def pair(state):
                t, _ = state
                j = nfull - 1 - 2 * t
                half_step(j, 1, True)
                half_step(j - 1, 0, True)
                return t + 1, stick_left(0, tq)

            npair = nfull // 2
            _, alive = lax.while_loop(lambda st: (st[0] < npair - 1) & (st[1] > 0), pair,
                                      (jnp.int32(0), jnp.int32(1)))

            @pl.when(alive > 0)
            def _():
                half_step(1, 1, True)
                half_step(0, 0, False)

        @pl.when((nfull > 0) & (left_rest == 0) & (left_first > 0))
        def _():
            def single(state):
                j, _ = state
                start = pl.multiple_of(j * dch, dch)
                kc = k_chunk(start, dch)
                for hh in range(2):
                    produce(hh, 0, kc, 0, dch, False)
                    consume(hh, 0, suffix_sums(hh, 0, 0, dch), start, 0, dch, False)
                return j - 1, stick_left(0, dch)

            lax.while_loop(lambda st: (st[0] >= 0) & (st[1] > 0), single, (nfull - 1, jnp.int32(1)))

        o_t = jnp.concatenate([acc_ref[0], acc_ref[1]], axis=0)

    o_ref[0] = o_t.T.astype(o_ref.dtype)


def _attention(qt, k, vt, kx, *, mode, extra, tq, ch, dch, bounds=None):
    b, s, width = k.shape
    heads = qt.shape[1]
    kd = qt.shape[2]
    in_specs = [pl.BlockSpec((1, 2, kd, tq), lambda bi, pi, qi: (bi, pi, 0, qi)),
                pl.BlockSpec((1, s, LANES), lambda bi, pi, qi: (bi, 0, pi))]
    args = [qt, k]
    key_tile = ch
    if extra:
        in_specs.append(pl.BlockSpec((1, s, LANES), lambda bi, pi, qi: (bi, 0, 0)))
        args.append(kx)
    if extra == "bias":
        in_specs.append(pl.BlockSpec((1,) + bounds.shape[1:], lambda bi, pi, qi: (bi, 0, 0, 0)))
        args.append(bounds)
        key_tile = s // bounds.shape[2]
        assert key_tile % ch == 0
    in_specs.append(pl.BlockSpec((1, LANES, s), lambda bi, pi, qi: (bi, pi, 0)))
    args.append(vt)
    acc_rows = HEAD_DIM + (SUM_ROWS if mode == "softmax" else 0)
    nslot = 2 + tq // dch
    stat_rows = 8 * pl.cdiv(2 + 2 * nslot, 8)
    scratch = [pltpu.VMEM((2, acc_rows, tq), F32), pltpu.VMEM((stat_rows, tq), F32)]
    if mode == "softmax":
        scratch.append(pltpu.VMEM((2 + tq // ch, 2, ch, tq), F32))
    else:
        upper = jnp.where(lax.broadcasted_iota(jnp.int32, (dch, dch), 1)
                          > lax.broadcasted_iota(jnp.int32, (dch, dch), 0), -1.0, 0.0).astype(BF16)
        in_specs.append(pl.BlockSpec((dch, dch), lambda bi, pi, qi: (0, 0)))
        args.append(upper)
        scratch += [pltpu.VMEM((nslot, 2, dch, tq), F32),
                    pltpu.VMEM((nslot, 2, dch, tq), BF16)]
    kernel = functools.partial(_attn_kernel, mode=mode, extra=extra, tq=tq, ch=ch, dch=dch, key_tile=key_tile)
    return pl.pallas_call(
        kernel,
        out_shape=jax.ShapeDtypeStruct((b, s, width), BF16),
        grid=(b, heads // 2, s // tq),
        in_specs=in_specs,
        out_specs=pl.BlockSpec((1, tq, LANES), lambda bi, pi, qi: (bi, qi, pi)),
        scratch_shapes=scratch,
        compiler_params=pltpu.CompilerParams(
            dimension_semantics=("arbitrary", "arbitrary", "arbitrary"), vmem_limit_bytes=VMEM_LIMIT),
        name="attn_" + mode + ("_" + extra if extra else ""),
    )(*args)


def _out_kernel(*refs, n_o):
    o_refs = refs[:n_o]
    gate_ref, x_ref, w_ref, g_ref, out_ref = refs[n_o:]
    out_ref[...] = _gated_out(o_refs, gate_ref, x_ref, w_ref, g_ref)


def _out_stage(os, gate, x, w, g, tm):
    n, d = x.shape
    os = [o.reshape(n, o.shape[-1]) for o in os]
    in_specs = [pl.BlockSpec((tm, o.shape[1]), lambda i: (i, 0)) for o in os]
    in_specs += [pl.BlockSpec((tm, gate.shape[1]), lambda i: (i, 0)),
                 pl.BlockSpec((tm, d), lambda i: (i, 0)),
                 pl.BlockSpec(w.shape, lambda i: (0, 0)),
                 pl.BlockSpec((1, d), lambda i: (0, 0))]
    return pl.pallas_call(
        functools.partial(_out_kernel, n_o=len(os)),
        out_shape=jax.ShapeDtypeStruct((n, d), F32),
        grid=(n // tm,),
        in_specs=in_specs,
        out_specs=pl.BlockSpec((tm, d), lambda i: (i, 0)),
        compiler_params=pltpu.CompilerParams(
            dimension_semantics=("arbitrary",), vmem_limit_bytes=VMEM_LIMIT),
        name="out_stage",
    )(*os, gate, x, w, g.reshape(1, d))


def kernel(x, positions, l0_pre_g, l0_post_g, l0_w_in, l0_q_a_g, l0_w_q_b, l0_kv_a_g, l0_w_kv_b, l0_w_out,
           l1_pre_g, l1_post_g, l1_w_in, l1_b_f, l1_w_out):
    b, s, d = x.shape
    tm = min(ROW_TILE, s)
    tq = min(Q_TILE, s)
    dch = min(DIAG_CHUNK, tq // 2)
    ch = min(SOFTMAX_CHUNK, tq // 2)
    assert tq % (2 * ch) == 0 and ch % dch == 0 and s % tq == 0
    sb_scale = HEAD_DIM ** -0.5
    mla_scale = (MLA_NOPE + MLA_ROPE) ** -0.5

    w = l0_w_in
    c0 = 4 * SB_WIDTH
    c1 = c0 + MLA_Q_LORA
    c2 = c1 + MLA_KV_LORA
    c3 = c2 + MLA_ROPE
    half = MLA_ROPE // 2
    wkr = w[:, c2:c3]
    wkr_rot = jnp.concatenate([-wkr[:, half:], wkr[:, :half]], axis=1)
    lane_pad = jnp.zeros((d, LANES - MLA_ROPE), F32)
    wn0 = jnp.concatenate([w[:, 512:1024], w[:, 1536:2048], w[:, c3:], w[:, c0:c1], w[:, c1:c2],
                           wkr, lane_pad, wkr_rot, lane_pad], axis=1).astype(BF16)
    wt0 = jnp.concatenate([w[:, 0:512] * sb_scale, w[:, 1024:1536]], axis=1).T.astype(BF16)
    wqbt = (l0_w_q_b * (mla_scale * LOG2E)).T.astype(BF16)
    wkv = l0_w_kv_b.reshape(MLA_KV_LORA, MLA_HEADS, MLA_NOPE + MLA_V)
    wk = wkv[:, :, :MLA_NOPE].reshape(MLA_KV_LORA, -1).astype(BF16)
    wvt = wkv[:, :, MLA_NOPE:].reshape(MLA_KV_LORA, -1).T.astype(BF16)
    invf = ROPE_THETA ** (-jnp.arange(0, MLA_ROPE, 2, dtype=F32) / MLA_ROPE)

    qt_sb, k_sb, vt_sb, gate0, qt_mla, k_mla, kx_mla, vt_mla = _l0_in(
        x, positions, l0_pre_g, wn0, wt0, l0_q_a_g, wqbt, l0_kv_a_g, wk, wvt, invf, tm)
    o_sb = _attention(qt_sb, k_sb, vt_sb, None, mode="stick", extra=None, tq=tq, ch=ch, dch=dch)
    o_mla = _attention(qt_mla, k_mla, vt_mla, kx_mla, mode="softmax", extra="rows", tq=tq, ch=ch, dch=dch)

    w1 = l1_w_in
    f_pad = jnp.zeros((d, LANES - FOX_HEADS), F32)
    wn1 = jnp.concatenate([w1[:, 1024:2048], w1[:, 3072:4096], w1[:, 4096:], f_pad], axis=1).astype(BF16)
    wt1 = jnp.concatenate([w1[:, 0:1024] * (sb_scale * LOG2E), w1[:, 2048:3072]], axis=1).T.astype(BF16)
    x1, qt1, k1, vt1, gate1, kx1, tile_bounds = _l1_in(
        o_sb, o_mla, gate0, x, l0_w_out.astype(BF16), l0_post_g, l1_pre_g, wn1, wt1, l1_b_f, tm)
    bounds = tile_bounds[:, :, 0:2, :].transpose(0, 2, 1, 3)
    o1 = _attention(qt1, k1, vt1, kx1, mode="softmax", extra="bias", tq=tq, ch=ch, dch=dch, bounds=bounds)
    out = _out_stage([o1], gate1.reshape(b * s, -1), x1.reshape(b * s, d), l1_w_out.astype(BF16),
                     l1_post_g, tm)
    return out.reshape(b, s, d)
```

```python
import functools
import math

import jax
import jax.numpy as jnp
from jax import lax
from jax.experimental import pallas as pl
from jax.experimental.pallas import tpu as pltpu

D_MODEL = 1024
RMS_EPS = 1e-6
HEAD_DIM = 64
SB_HEADS = 8
SB_WIDTH = SB_HEADS * HEAD_DIM
MLA_HEADS = 8
MLA_NOPE = 64
MLA_ROPE = 32
MLA_V = 64
MLA_Q_LORA = 384
MLA_KV_LORA = 256
MLA_WIDTH = MLA_HEADS * MLA_V
ROPE_THETA = 10000.0
FOX_HEADS = 16
FOX_WIDTH = FOX_HEADS * HEAD_DIM

LANES = 128
ROW_TILE = 512
Q_TILE = 1024
SOFTMAX_CHUNK = 512
DIAG_CHUNK = 256
SUM_ROWS = 16
MASK_VALUE = -1e30
STICK_DEAD = -104.0
SCORE_DEAD = -152.0
LOG2E = math.log2(math.e)
VMEM_LIMIT = 48 * 1024 * 1024
F32 = jnp.float32
BF16 = jnp.bfloat16


def _rms(xf, g):
    var = jnp.mean(xf * xf, axis=-1, keepdims=True)
    return xf * lax.rsqrt(var + RMS_EPS) * g


def _dot(a, b):
    return jnp.dot(a, b, preferred_element_type=F32)


def _dot_nt(a, b):
    return lax.dot_general(a, b, (((1,), (1,)), ((), ())), preferred_element_type=F32)


def _silu(g):
    return g / (1.0 + jnp.exp(-g))


def _gated_out(o_refs, gate_ref, x_ref, w_ref, g_ref):
    gate = _silu(gate_ref[...].astype(F32))
    y = None
    col = 0
    for o_ref in o_refs:
        wdt = o_ref.shape[1]
        og = (o_ref[...].astype(F32) * gate[:, col:col + wdt]).astype(BF16)
        part = _dot(og, w_ref[col:col + wdt, :])
        y = part if y is None else y + part
        col += wdt
    return x_ref[...] + _rms(y, g_ref[...])


def _split3(v):
    a1 = v.astype(BF16)
    r1 = v - a1.astype(F32)
    a2 = r1.astype(BF16)
    a3 = (r1 - a2.astype(F32)).astype(BF16)
    return a1, a2, a3


def _write_padded_heads(ref, rows_t, heads, width):
    tm = rows_t.shape[1]
    zeros = jnp.zeros((HEAD_DIM, tm), BF16)
    for hh in range(heads):
        off = HEAD_DIM * (hh % 2)
        ref[0, hh, off:off + HEAD_DIM, :] = rows_t[hh * width:hh * width + HEAD_DIM].astype(BF16)
        ref[0, hh, HEAD_DIM - off:2 * HEAD_DIM - off, :] = zeros


def _l0_in_kernel(x_ref, posr_ref, posc_ref, preg_ref, wn_ref, wt_ref, qag_ref, wqbt_ref,
                  kvag_ref, wk_ref, wvt_ref, invfr_ref, invfc_ref,
                  qtsb_ref, ksb_ref, vtsb_ref, gate_ref, qtmla_ref, kmla_ref, kxmla_ref, vtmla_ref):
    x = x_ref[0]
    h = _rms(x, preg_ref[...]).astype(BF16)

    def nat(a, b):
        return _dot(h, wn_ref[:, a:b])

    ksb_ref[0] = nat(0, 512).astype(BF16)
    gate_ref[0] = nat(512, 1536).astype(BF16)
    qa = nat(1536, 1920)
    ckv = nat(1920, 2176)
    kr = nat(2176, 2304)
    krr = nat(2304, 2432)

    tr = _dot_nt(wt_ref[...], h)
    _write_padded_heads(qtsb_ref, tr[0:SB_WIDTH], SB_HEADS, HEAD_DIM)
    vtsb_ref[0] = tr[SB_WIDTH:2 * SB_WIDTH].astype(BF16)

    qan = _rms(qa, qag_ref[...]).astype(BF16)
    qt = _dot_nt(wqbt_ref[...], qan)
    angt = invfc_ref[...] * posr_ref[0].astype(F32)
    cost = jnp.cos(angt)
    sint = jnp.sin(angt)
    tm = qt.shape[1]
    qtmla_ref[...] = jnp.zeros(qtmla_ref.shape, BF16)
    per_head = MLA_NOPE + MLA_ROPE
    half = MLA_ROPE // 2
    for hh in range(MLA_HEADS):
        base = hh * per_head
        off = MLA_NOPE * (hh % 2)
        qtmla_ref[0, hh, off:off + MLA_NOPE, :] = qt[base:base + MLA_NOPE].astype(BF16)
        x1 = qt[base + MLA_NOPE:base + MLA_NOPE + half]
        x2 = qt[base + MLA_NOPE + half:base + per_head]
        qtmla_ref[0, hh, LANES:LANES + half, :] = (x1 * cost - x2 * sint).astype(BF16)
        qtmla_ref[0, hh, LANES + half:LANES + MLA_ROPE, :] = (x2 * cost + x1 * sint).astype(BF16)

    ckvn = _rms(ckv, kvag_ref[...]).astype(BF16)
    kmla_ref[0] = _dot(ckvn, wk_ref[...]).astype(BF16)
    vtmla_ref[0] = _dot_nt(wvt_ref[...], ckvn).astype(BF16)
    ang = posc_ref[0].astype(F32) * invfr_ref[...]
    kxmla_ref[0] = (kr * jnp.cos(ang) + krr * jnp.sin(ang)).astype(BF16)


def _l0_in(x, positions, pre_g, wn, wt, qag, wqbt, kvag, wk, wvt, invf, tm):
    b, s, d = x.shape
    nt = s // tm
    posr = positions.reshape(b, 1, s)
    posc = positions.reshape(b, s, 1)
    invfr = jnp.tile(invf, LANES // invf.shape[0]).reshape(1, LANES)
    invfc = invf.reshape(-1, 1)
    const = lambda shape: pl.BlockSpec(shape, lambda bi, ti: (0,) * len(shape))
    row = lambda w: pl.BlockSpec((1, tm, w), lambda bi, ti: (bi, ti, 0))
    colt = lambda r: pl.BlockSpec((1, r, tm), lambda bi, ti: (bi, 0, ti))
    headt = lambda hn, r: pl.BlockSpec((1, hn, r, tm), lambda bi, ti: (bi, 0, 0, ti))
    out_shape = (
        jax.ShapeDtypeStruct((b, SB_HEADS, LANES, s), BF16),
        jax.ShapeDtypeStruct((b, s, SB_WIDTH), BF16),
        jax.ShapeDtypeStruct((b, SB_WIDTH, s), BF16),
        jax.ShapeDtypeStruct((b, s, SB_WIDTH + MLA_WIDTH), BF16),
        jax.ShapeDtypeStruct((b, MLA_HEADS, 2 * LANES, s), BF16),
        jax.ShapeDtypeStruct((b, s, MLA_WIDTH), BF16),
        jax.ShapeDtypeStruct((b, s, LANES), BF16),
        jax.ShapeDtypeStruct((b, MLA_WIDTH, s), BF16),
    )
    return pl.pallas_call(
        _l0_in_kernel,
        out_shape=out_shape,
        grid=(b, nt),
        in_specs=[
            row(d),
            pl.BlockSpec((1, 1, tm), lambda bi, ti: (bi, 0, ti)),
            pl.BlockSpec((1, tm, 1), lambda bi, ti: (bi, ti, 0)),
            const((1, d)), const(wn.shape), const(wt.shape), const((1, MLA_Q_LORA)), const(wqbt.shape),
            const((1, MLA_KV_LORA)), const(wk.shape), const(wvt.shape),
            const((1, LANES)), const((MLA_ROPE // 2, 1)),
        ],
        out_specs=(headt(SB_HEADS, LANES), row(SB_WIDTH), colt(SB_WIDTH), row(SB_WIDTH + MLA_WIDTH),
                   headt(MLA_HEADS, 2 * LANES), row(MLA_WIDTH), row(LANES), colt(MLA_WIDTH)),
        compiler_params=pltpu.CompilerParams(
            dimension_semantics=("arbitrary", "arbitrary"), vmem_limit_bytes=VMEM_LIMIT),
        name="l0_in",
    )(x, posr, posc, pre_g.reshape(1, d), wn, wt, qag.reshape(1, -1), wqbt,
      kvag.reshape(1, -1), wk, wvt, invfr, invfc)


def _l1_in_kernel(oa_ref, ob_ref, gate0_ref, x_ref, wout_ref, postg_ref,
                  preg_ref, wn_ref, wt_ref, bf_ref, tri_ref, place_ref, group_ref,
                  x1_ref, qt_ref, k_ref, vt_ref, gate_ref, kx_ref, bounds_ref, carry_ref):
    @pl.when(pl.program_id(1) == 0)
    def _():
        carry_ref[...] = jnp.zeros(carry_ref.shape, F32)

    x = _gated_out([oa_ref.at[0], ob_ref.at[0]], gate0_ref.at[0], x_ref.at[0], wout_ref, postg_ref)
    x1_ref[0] = x
    h = _rms(x, preg_ref[...]).astype(BF16)
    kb = _dot(h, wn_ref[:, 0:1024]).astype(BF16)
    k_ref[0] = kb
    gate_ref[0] = _dot(h, wn_ref[:, 1024:2048]).astype(BF16)
    f = _dot(h, wn_ref[:, 2048:2176])[:, 0:FOX_HEADS] + bf_ref[...]

    tr = _dot_nt(wt_ref[...], h)
    _write_padded_heads(qt_ref, tr[0:FOX_WIDTH], FOX_HEADS, HEAD_DIM)
    vt_ref[0] = tr[FOX_WIDTH:2 * FOX_WIDTH].astype(BF16)

    logf = jnp.minimum(f, 0.0) - jnp.log1p(jnp.exp(-jnp.abs(f)))
    tri = tri_ref[...]
    c = carry_ref[0:1, 0:FOX_HEADS]
    for term in _split3(logf):
        c = c + _dot(tri, term)
    tm = c.shape[0]
    carry_ref[0:1, 0:FOX_HEADS] = c[tm - 1:tm, :]
    slab = jnp.zeros((tm, LANES), F32)
    for t, term in enumerate(_split3(-LOG2E * c)):
        slab = slab + _dot(term, place_ref[t])
    kx_ref[0] = slab.astype(BF16)
    kf = kb.astype(F32)
    top = jnp.broadcast_to(jnp.max(kf * kf, axis=0, keepdims=True), (8, FOX_WIDTH)).astype(BF16)
    ksq = _dot(top, group_ref[...])
    bounds_ref[0, 0] = jnp.concatenate(
        [ksq[0:1, :], jnp.max(slab, axis=0, keepdims=True), jnp.zeros((6, LANES), F32)], axis=0)


def _l1_in(o_a, o_b, gate0, x, w_out0, post_g0, pre_g, wn, wt, bf, tm):
    b, s, d = x.shape
    nt = s // tm
    tri = (lax.broadcasted_iota(jnp.int32, (tm, tm), 1)
           <= lax.broadcasted_iota(jnp.int32, (tm, tm), 0)).astype(BF16)
    hh = lax.broadcasted_iota(jnp.int32, (3, FOX_HEADS, LANES), 1)
    tt = lax.broadcasted_iota(jnp.int32, (3, FOX_HEADS, LANES), 0)
    ll = lax.broadcasted_iota(jnp.int32, (3, FOX_HEADS, LANES), 2)
    place = (ll == tt * FOX_HEADS + hh).astype(BF16)
    group = (lax.broadcasted_iota(jnp.int32, (FOX_WIDTH, LANES), 0) // HEAD_DIM
             == lax.broadcasted_iota(jnp.int32, (FOX_WIDTH, LANES), 1)).astype(BF16)
    const = lambda shape: pl.BlockSpec(shape, lambda bi, ti: (0,) * len(shape))
    row = lambda w: pl.BlockSpec((1, tm, w), lambda bi, ti: (bi, ti, 0))
    out_shape = (
        jax.ShapeDtypeStruct((b, s, d), F32),
        jax.ShapeDtypeStruct((b, FOX_HEADS, LANES, s), BF16),
        jax.ShapeDtypeStruct((b, s, FOX_WIDTH), BF16),
        jax.ShapeDtypeStruct((b, FOX_WIDTH, s), BF16),
        jax.ShapeDtypeStruct((b, s, FOX_WIDTH), BF16),
        jax.ShapeDtypeStruct((b, s, LANES), BF16),
        jax.ShapeDtypeStruct((b, nt, 8, LANES), F32),
    )
    return pl.pallas_call(
        _l1_in_kernel,
        out_shape=out_shape,
        grid=(b, nt),
        in_specs=[row(o_a.shape[2]), row(o_b.shape[2]), row(gate0.shape[2]), row(d),
                  const(w_out0.shape), const((1, d)),
                  const((1, d)), const(wn.shape), const(wt.shape),
                  const((1, FOX_HEADS)), const((tm, tm)), const(place.shape), const(group.shape)],
        out_specs=(row(d),
                   pl.BlockSpec((1, FOX_HEADS, LANES, tm), lambda bi, ti: (bi, 0, 0, ti)),
                   row(FOX_WIDTH),
                   pl.BlockSpec((1, FOX_WIDTH, tm), lambda bi, ti: (bi, 0, ti)),
                   row(FOX_WIDTH), row(LANES),
                   pl.BlockSpec((1, 1, 8, LANES), lambda bi, ti: (bi, ti, 0, 0))),
        scratch_shapes=[pltpu.VMEM((8, LANES), F32)],
        compiler_params=pltpu.CompilerParams(
            dimension_semantics=("arbitrary", "arbitrary"), vmem_limit_bytes=VMEM_LIMIT),
        name="l1_in",
    )(o_a, o_b, gate0, x, w_out0, post_g0.reshape(1, d),
      pre_g.reshape(1, d), wn, wt, bf.reshape(1, -1), tri, place, group)


def _attn_pair_kernel(*refs, mode, extra, tq, nq, **kw):
    n_in = 3 + (1 if extra else 0) + (1 if extra == "bias" else 0) + (1 if mode == "stick" else 0)

    def tile(qi, carry):
        q0 = pl.multiple_of(qi * tq, tq)
        sub = list(refs)
        sub[0] = refs[0].at[:, :, :, pl.ds(q0, tq)]
        sub[n_in] = refs[n_in].at[:, pl.ds(q0, tq), :]
        _attn_kernel(*sub, mode=mode, extra=extra, tq=tq, q0=q0, **kw)
        return carry

    lax.fori_loop(0, nq, tile, 0)


def _attn_kernel(*refs, mode, extra, tq, ch, dch, key_tile, q0):
    refs = list(refs)
    qt_ref = refs.pop(0)
    k_ref = refs.pop(0)
    kx_ref = refs.pop(0) if extra else None
    tab_ref = refs.pop(0) if extra == "bias" else None
    vt_ref = refs.pop(0)
    u_ref = refs.pop(0) if mode == "stick" else None
    o_ref, acc_ref, stat_ref, sbuf_ref = refs[:4]

    pi = pl.program_id(1)
    nd = tq // dch

    qts = []
    for hh in range(2):
        qt = qt_ref[0, hh]
        if extra == "bias":
            hg = 2 * pi + hh
            r = lax.broadcasted_iota(jnp.int32, (LANES, tq), 0)
            hit = (r == hg) | (r == hg + FOX_HEADS) | (r == hg + 2 * FOX_HEADS)
            qt = jnp.concatenate([qt, jnp.where(hit, 1.0, 0.0).astype(BF16)], axis=0)
        qts.append(qt)

    def k_chunk(start, n):
        k = k_ref[0, pl.ds(start, n), :]
        if extra:
            k = jnp.concatenate([k, kx_ref[0, pl.ds(start, n), :]], axis=1)
        return k

    def v_chunk(hh, start, n):
        return vt_ref[0, hh * HEAD_DIM:(hh + 1) * HEAD_DIM, pl.ds(start, n)]

    acc_ref[...] = jnp.zeros(acc_ref.shape, F32)
    rr = lax.broadcasted_iota(jnp.int32, (dch, dch), 0)
    cc = lax.broadcasted_iota(jnp.int32, (dch, dch), 1)

    def diag_ranges(c):
        off = c * dch
        out = [(off, off + dch, True)]
        if off + dch < tq:
            out.append((off + dch, tq, False))
        return out

    if mode == "softmax":
        nfull = q0 // ch
        per = ch // dch
        nb = tq // ch

        def with_ones(v):
            return jnp.concatenate([v, jnp.ones((SUM_ROWS, v.shape[1]), BF16)], axis=0)

        def produce(hh, slot, k):
            s = _dot(k, qts[hh])
            sbuf_ref[slot, hh] = s
            stat_ref[4 * hh + 2 + slot:4 * hh + 3 + slot, :] = jnp.max(s, axis=0, keepdims=True)

        for g in range(nb):
            kb = k_chunk(pl.multiple_of(q0 + g * ch, ch), ch)
            for hh in range(2):
                sbuf_ref[2 + g, hh, :, g * ch:] = _dot(kb, qts[hh][:, g * ch:])
        kl = k_chunk(pl.multiple_of(jnp.maximum(nfull - 1, 0) * ch, ch), ch)
        for hh in range(2):
            produce(hh, 1, kl)

        def band_scores(hh, c, c0, c1, tri):
            g, r = divmod(c, per)
            s = sbuf_ref[2 + g, hh, r * dch:(r + 1) * dch, c0:c1]
            return jnp.where(rr <= cc, s, MASK_VALUE) if tri else s

        for hh in range(2):
            bmax = [None] * nd
            for c in range(nd):
                for c0, c1, tri in diag_ranges(c):
                    cm = jnp.max(band_scores(hh, c, c0, c1, tri), axis=0, keepdims=True)
                    for blk in range(c0 // dch, c1 // dch):
                        part = cm[:, blk * dch - c0:(blk + 1) * dch - c0]
                        bmax[blk] = part if bmax[blk] is None else jnp.maximum(bmax[blk], part)
            m_band = jnp.concatenate(bmax, axis=1)
            stat_ref[4 * hh:4 * hh + 1, :] = m_band
            for c in range(nd):
                off = c * dch
                ps = [jnp.exp2(band_scores(hh, c, c0, c1, tri) - m_band[:, c0:c1])
                      for c0, c1, tri in diag_ranges(c)]
                p = ps[0] if len(ps) == 1 else jnp.concatenate(ps, axis=1)
                pv = _dot(with_ones(v_chunk(hh, pl.multiple_of(q0 + off, dch), dch)), p.astype(BF16))
                if c == 0:
                    acc_ref[hh] = pv
                else:
                    acc_ref[hh, :, off:] = acc_ref[hh, :, off:] + pv

        jlo = 0
        if extra == "bias":
            tab_k = tab_ref[0, 0]
            tab_b = tab_ref[0, 1]
            tiles = tab_k.shape[0]
            lane = lax.broadcasted_iota(jnp.int32, tab_k.shape, 1)
            dead = None
            for hh in range(2):
                pick = lane == 2 * pi + hh
                k_hi = jnp.sqrt(jnp.sum(jnp.where(pick, tab_k, 0.0), axis=1, keepdims=True) * 1.02)
                b_top = jnp.sum(jnp.where(pick, tab_b, 0.0), axis=1, keepdims=True)
                b_hi = b_top + jnp.abs(b_top) * (2.0 ** -7)
                qf = qts[hh][0:LANES, :].astype(F32)
                q_hi = jnp.sqrt(jnp.max(jnp.sum(qf * qf, axis=0, keepdims=True), axis=1, keepdims=True) * 1.0001)
                m_lo = jnp.min(stat_ref[4 * hh:4 * hh + 1, :], axis=1, keepdims=True)
                gone = k_hi * q_hi + b_hi - m_lo <= SCORE_DEAD
                dead = gone if dead is None else dead & gone
            tile_idx = lax.broadcasted_iota(jnp.int32, (tiles, 1), 0)
            first_live = jnp.min(jnp.where(dead, tiles, tile_idx))
            jlo = jnp.minimum(first_live * (key_tile // ch), nfull)

        def half_step(j, slot, prefetch):
            start = pl.multiple_of(j * ch, ch)
            if prefetch:
                kn = k_chunk(pl.multiple_of(start - ch, ch), ch)
            for hh in range(2):
                if prefetch:
                    produce(hh, 1 - slot, kn)
                m_prev = stat_ref[4 * hh:4 * hh + 1, :]
                m_new = jnp.maximum(m_prev, stat_ref[4 * hh + 2 + slot:4 * hh + 3 + slot, :])
                stat_ref[4 * hh:4 * hh + 1, :] = m_new
                p = jnp.exp2(sbuf_ref[slot, hh] - m_new)
                pv = _dot(with_ones(v_chunk(hh, start, ch)), p.astype(BF16))
                acc_ref[hh] = jnp.exp2(m_prev - m_new) * acc_ref[hh] + pv

        def body(t, carry):
            j = nfull - 1 - 2 * t
            half_step(j, 1, True)
            half_step(j - 1, 0, True)
            return carry

        nleft = nfull - jlo
        lax.fori_loop(0, (nleft + 1) // 2 - 1, body, 0)

        @pl.when((nleft > 0) & (lax.rem(nleft, 2) == 0))
        def _():
            half_step(jlo + 1, 1, True)
            half_step(jlo, 0, False)

        @pl.when(lax.rem(nleft, 2) == 1)
        def _():
            half_step(jlo, 1, False)

        o_t = jnp.concatenate([acc_ref[hh, 0:HEAD_DIM, :] / acc_ref[hh, HEAD_DIM:HEAD_DIM + 1, :]
                               for hh in range(2)], axis=0)
    else:
        hl_ref = refs[4]
        nfull = q0 // dch
        stat_ref[...] = jnp.zeros(stat_ref.shape, F32)

        def mask_first(x, fill):
            first = jnp.where(rr < cc, x[:, 0:dch], fill)
            return first if x.shape[1] == dch else jnp.concatenate([first, x[:, dch:]], axis=1)

        def produce(hh, slot, k, lo, hi, diag):
            z = _dot(k, qts[hh][:, lo:hi])
            sp = jnp.maximum(z, 0.0) + jnp.log(1.0 + jnp.exp2(jnp.abs(z) * -LOG2E))
            if diag:
                sp = mask_first(sp, 0.0)
            sbuf_ref[slot, hh, :, lo:hi] = z - sp
            hl_ref[slot, hh, :, lo:hi] = sp.astype(BF16)
            stat_ref[2 + 2 * slot + hh:3 + 2 * slot + hh, lo:hi] = sp[0:1, :]

        def suffix_sums(hh, slot, lo, hi):
            u = u_ref[...]
            return _dot(u, hl_ref[slot, hh, :, lo:hi])

        def consume(hh, slot, remain, start, lo, hi, diag):
            carry = stat_ref[hh:hh + 1, lo:hi]
            first = stat_ref[2 + 2 * slot + hh:3 + 2 * slot + hh, lo:hi]
            stat_ref[hh:hh + 1, lo:hi] = carry + remain[0:1, :] - first
            w = jnp.exp(sbuf_ref[slot, hh, :, lo:hi] + remain + carry)
            if diag:
                w = mask_first(w, 0.0)
            acc_ref[hh, :, lo:hi] = acc_ref[hh, :, lo:hi] + _dot(v_chunk(hh, start, dch), w.astype(BF16))

        for c in reversed(range(nd)):
            kc = k_chunk(pl.multiple_of(q0 + c * dch, dch), dch)
            for hh in range(2):
                produce(hh, 2 + c, kc, c * dch, tq, True)
        ahead = [suffix_sums(hh, 1 + nd, (nd - 1) * dch, tq) for hh in range(2)]
        for c in reversed(range(nd)):
            remain = ahead
            if c > 0:
                ahead = [suffix_sums(hh, 1 + c, (c - 1) * dch, tq) for hh in range(2)]
            for hh in range(2):
                consume(hh, 2 + c, remain[hh], pl.multiple_of(q0 + c * dch, dch), c * dch, tq, True)

        def stick_left(lo, hi):
            return (jnp.max(stat_ref[0:2, lo:hi]) > STICK_DEAD).astype(jnp.int32)

        left_first = stick_left(0, dch)
        left_rest = stick_left(dch, tq)

        def half_step(j, slot, prefetch):
            start = pl.multiple_of(j * dch, dch)
            if prefetch:
                kn = k_chunk(pl.multiple_of((j - 1) * dch, dch), dch)
            for hh in range(2):
                remain = suffix_sums(hh, slot, 0, tq)
                if prefetch:
                    produce(hh, 1 - slot, kn, 0, tq, False)
                consume(hh, slot, remain, start, 0, tq, False)

        @pl.when((nfull > 0) & (left_rest > 0))
        def _():
            kl = k_chunk(pl.multiple_of((nfull - 1) * dch, dch), dch)
            for hh in range(2):
                produce(hh, 1, kl, 0, tq, False)

            def pair(state):
                t, _ = state
                j = nfull - 1 - 2 * t
                half_step(j, 1, True)
                half_step(j - 1, 0, True)
                return t + 1, stick_left(0, tq)

            npair = nfull // 2
            _, alive = lax.while_loop(lambda st: (st[0] < npair - 1) & (st[1] > 0), pair,
                                      (jnp.int32(0), jnp.int32(1)))

            @pl.when(alive > 0)
            def _():
                half_step(1, 1, True)
                half_step(0, 0, False)

        @pl.when((nfull > 0) & (left_rest == 0) & (left_first > 0))
        def _():
            def single(state):
                j, _ = state
                start = pl.multiple_of(j * dch, dch)
                kc = k_chunk(start, dch)
                for hh in range(2):
                    produce(hh, 0, kc, 0, dch, False)
                    consume(hh, 0, suffix_sums(hh, 0, 0, dch), start, 0, dch, False)
                return j - 1, stick_left(0, dch)

            lax.while_loop(lambda st: (st[0] >= 0) & (st[1] > 0), single, (nfull - 1, jnp.int32(1)))

        o_t = jnp.concatenate([acc_ref[0], acc_ref[1]], axis=0)

    o_ref[0] = o_t.T.astype(o_ref.dtype)


def _attention(qt, k, vt, kx, *, mode, extra, tq, ch, dch, bounds=None):
    b, s, width = k.shape
    heads = qt.shape[1]
    kd = qt.shape[2]
    in_specs = [pl.BlockSpec((1, 2, kd, s), lambda bi, pi: (bi, pi, 0, 0)),
                pl.BlockSpec((1, s, LANES), lambda bi, pi: (bi, 0, pi))]
    args = [qt, k]
    key_tile = ch
    if extra:
        in_specs.append(pl.BlockSpec((1, s, LANES), lambda bi, pi: (bi, 0, 0)))
        args.append(kx)
    if extra == "bias":
        in_specs.append(pl.BlockSpec((1,) + bounds.shape[1:], lambda bi, pi: (bi, 0, 0, 0)))
        args.append(bounds)
        key_tile = s // bounds.shape[2]
        assert key_tile % ch == 0
    in_specs.append(pl.BlockSpec((1, LANES, s), lambda bi, pi: (bi, pi, 0)))
    args.append(vt)
    acc_rows = HEAD_DIM + (SUM_ROWS if mode == "softmax" else 0)
    nslot = 2 + tq // dch
    stat_rows = 8 * pl.cdiv(2 + 2 * nslot, 8)
    scratch = [pltpu.VMEM((2, acc_rows, tq), F32), pltpu.VMEM((stat_rows, tq), F32)]
    if mode == "softmax":
        scratch.append(pltpu.VMEM((2 + tq // ch, 2, ch, tq), F32))
    else:
        upper = jnp.where(lax.broadcasted_iota(jnp.int32, (dch, dch), 1)
                          > lax.broadcasted_iota(jnp.int32, (dch, dch), 0), -1.0, 0.0).astype(BF16)
        in_specs.append(pl.BlockSpec((dch, dch), lambda bi, pi: (0, 0)))
        args.append(upper)
        scratch += [pltpu.VMEM((nslot, 2, dch, tq), F32),
                    pltpu.VMEM((nslot, 2, dch, tq), BF16)]
    kernel = functools.partial(_attn_pair_kernel, mode=mode, extra=extra, tq=tq, nq=s // tq, ch=ch, dch=dch,
                               key_tile=key_tile)
    return pl.pallas_call(
        kernel,
        out_shape=jax.ShapeDtypeStruct((b, s, width), BF16),
        grid=(b, heads // 2),
        in_specs=in_specs,
        out_specs=pl.BlockSpec((1, s, LANES), lambda bi, pi: (bi, 0, pi)),
        scratch_shapes=scratch,
        compiler_params=pltpu.CompilerParams(
            dimension_semantics=("arbitrary", "arbitrary"), vmem_limit_bytes=VMEM_LIMIT),
        name="attn_" + mode + ("_" + extra if extra else ""),
    )(*args)


def _out_kernel(*refs, n_o):
    o_refs = refs[:n_o]
    gate_ref, x_ref, w_ref, g_ref, out_ref = refs[n_o:]
    out_ref[...] = _gated_out(o_refs, gate_ref, x_ref, w_ref, g_ref)


def _out_stage(os, gate, x, w, g, tm):
    n, d = x.shape
    os = [o.reshape(n, o.shape[-1]) for o in os]
    in_specs = [pl.BlockSpec((tm, o.shape[1]), lambda i: (i, 0)) for o in os]
    in_specs += [pl.BlockSpec((tm, gate.shape[1]), lambda i: (i, 0)),
                 pl.BlockSpec((tm, d), lambda i: (i, 0)),
                 pl.BlockSpec(w.shape, lambda i: (0, 0)),
                 pl.BlockSpec((1, d), lambda i: (0, 0))]
    return pl.pallas_call(
        functools.partial(_out_kernel, n_o=len(os)),
        out_shape=jax.ShapeDtypeStruct((n, d), F32),
        grid=(n // tm,),
        in_specs=in_specs,
        out_specs=pl.BlockSpec((tm, d), lambda i: (i, 0)),
        compiler_params=pltpu.CompilerParams(
            dimension_semantics=("arbitrary",), vmem_limit_bytes=VMEM_LIMIT),
        name="out_stage",
    )(*os, gate, x, w, g.reshape(1, d))


def kernel(x, positions, l0_pre_g, l0_post_g, l0_w_in, l0_q_a_g, l0_w_q_b, l0_kv_a_g, l0_w_kv_b, l0_w_out,
           l1_pre_g, l1_post_g, l1_w_in, l1_b_f, l1_w_out):
    b, s, d = x.shape
    tm = min(ROW_TILE, s)
    tq = min(Q_TILE, s)
    dch = min(DIAG_CHUNK, tq // 2)
    ch = min(SOFTMAX_CHUNK, tq // 2)
    assert tq % (2 * ch) == 0 and ch % dch == 0 and s % tq == 0
    sb_scale = HEAD_DIM ** -0.5
    mla_scale = (MLA_NOPE + MLA_ROPE) ** -0.5

    w = l0_w_in
    c0 = 4 * SB_WIDTH
    c1 = c0 + MLA_Q_LORA
    c2 = c1 + MLA_KV_LORA
    c3 = c2 + MLA_ROPE
    half = MLA_ROPE // 2
    wkr = w[:, c2:c3]
    wkr_rot = jnp.concatenate([-wkr[:, half:], wkr[:, :half]], axis=1)
    lane_pad = jnp.zeros((d, LANES - MLA_ROPE), F32)
    wn0 = jnp.concatenate([w[:, 512:1024], w[:, 1536:2048], w[:, c3:], w[:, c0:c1], w[:, c1:c2],
                           wkr, lane_pad, wkr_rot, lane_pad], axis=1).astype(BF16)
    wt0 = jnp.concatenate([w[:, 0:512] * sb_scale, w[:, 1024:1536]], axis=1).T.astype(BF16)
    wqbt = (l0_w_q_b * (mla_scale * LOG2E)).T.astype(BF16)
    wkv = l0_w_kv_b.reshape(MLA_KV_LORA, MLA_HEADS, MLA_NOPE + MLA_V)
    wk = wkv[:, :, :MLA_NOPE].reshape(MLA_KV_LORA, -1).astype(BF16)
    wvt = wkv[:, :, MLA_NOPE:].reshape(MLA_KV_LORA, -1).T.astype(BF16)
    invf = ROPE_THETA ** (-jnp.arange(0, MLA_ROPE, 2, dtype=F32) / MLA_ROPE)

    qt_sb, k_sb, vt_sb, gate0, qt_mla, k_mla, kx_mla, vt_mla = _l0_in(
        x, positions, l0_pre_g, wn0, wt0, l0_q_a_g, wqbt, l0_kv_a_g, wk, wvt, invf, tm)
    o_sb = _attention(qt_sb, k_sb, vt_sb, None, mode="stick", extra=None, tq=tq, ch=ch, dch=dch)
    o_mla = _attention(qt_mla, k_mla, vt_mla, kx_mla, mode="softmax", extra="rows", tq=tq, ch=ch, dch=dch)

    w1 = l1_w_in
    f_pad = jnp.zeros((d, LANES - FOX_HEADS), F32)
    wn1 = jnp.concatenate([w1[:, 1024:2048], w1[:, 3072:4096], w1[:, 4096:], f_pad], axis=1).astype(BF16)
    wt1 = jnp.concatenate([w1[:, 0:1024] * (sb_scale * LOG2E), w1[:, 2048:3072]], axis=1).T.astype(BF16)
    x1, qt1, k1, vt1, gate1, kx1, tile_bounds = _l1_in(
        o_sb, o_mla, gate0, x, l0_w_out.astype(BF16), l0_post_g, l1_pre_g, wn1, wt1, l1_b_f, tm)
    bounds = tile_bounds[:, :, 0:2, :].transpose(0, 2, 1, 3)
    o1 = _attention(qt1, k1, vt1, kx1, mode="softmax", extra="bias", tq=tq, ch=ch, dch=dch, bounds=bounds)
    out = _out_stage([o1], gate1.reshape(b * s, -1), x1.reshape(b * s, d), l1_w_out.astype(BF16),
                     l1_post_g, tm)
    return out.reshape(b, s, d)
```
